```python
import jax, jax.numpy as jnp
from jax import lax
import numpy as np

D_MODEL = 1024
BATCH = 2
SEQ = 8192
DEPTH = 1
DEC_BATCH = 16
DEC_SEQ = 16
PAST_LEN = 2048

CHUNK = 64
MIX_WIDTH = D_MODEL
POOL_WIDTH = MIX_WIDTH // 2
POOL_GROUPS = 4
POOL_GROUP_DIM = POOL_WIDTH // POOL_GROUPS
POOL_WINDOWS = (2, 4, 8, 16)
POOL_HIST = max(POOL_WINDOWS) - 1
GMLP_WIDTH = MIX_WIDTH - POOL_WIDTH
GMLP_HEADS = 4
GMLP_HEAD_DIM = GMLP_WIDTH // GMLP_HEADS
GMLP_CHUNK = 128
PEER_HEADS = 8
PEER_NKEYS = 128
PEER_EXPERTS = PEER_NKEYS * PEER_NKEYS
PEER_TOPK = 16
PEER_DKEY = 256
PEER_DHALF = PEER_DKEY // 2
PEER_BLOCK = 256
N_MOD = 6
EPS = 1e-6

kernel_name = "hybrid_pool_gmlp_peer_stream_step"


def rmsnorm(x, g):
    xf = x.astype(jnp.float32)
    y = xf * lax.rsqrt(jnp.mean(xf * xf, axis=-1, keepdims=True) + EPS)
    return (y * g.astype(jnp.float32)).astype(x.dtype)


def modulate(h, shift, scale):
    return h * (1 + scale[:, None, :]) + shift[:, None, :]


def pool_mixer(p, hist, pos0, pool_w, pool_scale):
    B, T, _ = p.shape
    padded = jnp.concatenate([hist, p], axis=1).astype(jnp.float32)
    csum = jnp.cumsum(padded, axis=1)
    csum = jnp.concatenate([jnp.zeros_like(csum[:, :1]), csum], axis=1)
    end = csum[:, POOL_HIST + 1:]
    pos = pos0 + jnp.arange(T)
    means = []
    for g, w in enumerate(POOL_WINDOWS):
        sl = slice(g * POOL_GROUP_DIM, (g + 1) * POOL_GROUP_DIM)
        start = csum[:, POOL_HIST + 1 - w:POOL_HIST + 1 - w + T, sl]
        cnt = jnp.minimum(pos + 1, w).astype(jnp.float32)[None, :, None]
        means.append((end[..., sl] - start) / cnt)
    diff = jnp.concatenate(means, axis=-1) - p.astype(jnp.float32)
    diff = diff.reshape(B, T, POOL_GROUPS, POOL_GROUP_DIM).astype(p.dtype)
    out = jnp.einsum('btgc,gcd->btgd', diff, pool_w).reshape(B, T, POOL_WIDTH)
    return out * pool_scale


def gmlp_mixer(u, v, norm_g, ws, bs):
    B, T, _ = v.shape
    vn = rmsnorm(v.reshape(B, T, GMLP_HEADS, GMLP_HEAD_DIM), norm_g.reshape(GMLP_HEADS, GMLP_HEAD_DIM))
    n_chunk = -(-T // GMLP_CHUNK)
    pad = n_chunk * GMLP_CHUNK - T
    vp = jnp.pad(vn, ((0, 0), (0, pad), (0, 0), (0, 0))).reshape(B, n_chunk, GMLP_CHUNK, GMLP_HEADS, GMLP_HEAD_DIM)
    mask = jnp.tril(jnp.ones((GMLP_CHUNK, GMLP_CHUNK), dtype=bool))
    wm = jnp.where(mask[None], ws, jnp.zeros_like(ws))
    s = jnp.einsum('hij,bnjhc->bnihc', wm, vp) + bs.T[None, None, :, :, None]
    s = s.reshape(B, n_chunk * GMLP_CHUNK, GMLP_HEADS, GMLP_HEAD_DIM)[:, :T].reshape(B, T, GMLP_WIDTH)
    return u * s, vn.reshape(B, T, GMLP_WIDTH)


def peer_retrieve(h, wq, keys):
    N = h.shape[0]
    q = (h @ wq).reshape(N, PEER_HEADS, 2, PEER_DHALF).astype(jnp.float32)
    s = jnp.einsum('nhpc,hpkc->nhpk', q, keys.astype(jnp.float32))
    sv, si = lax.top_k(s, PEER_TOPK)
    comb = sv[:, :, 0, :, None] + sv[:, :, 1, None, :]
    cv, ci = lax.top_k(comb.reshape(N, PEER_HEADS, PEER_TOPK * PEER_TOPK), PEER_TOPK)
    i1 = jnp.take_along_axis(si[:, :, 0], ci // PEER_TOPK, axis=-1)
    i2 = jnp.take_along_axis(si[:, :, 1], ci % PEER_TOPK, axis=-1)
    expert = i1 * PEER_NKEYS + i2
    gate = jax.nn.softmax(cv, axis=-1)
    return expert, gate


def peer_apply(h, expert, gate, u_tab, v_tab):
    N, D = h.shape
    blk = min(PEER_BLOCK, N)
    nb = -(-N // blk)
    pad = nb * blk - N
    hp = jnp.pad(h, ((0, pad), (0, 0))).reshape(nb, blk, D)
    ep = jnp.pad(expert, ((0, pad), (0, 0), (0, 0))).reshape(nb, blk, PEER_HEADS, PEER_TOPK)
    gp = jnp.pad(gate.astype(h.dtype), ((0, pad), (0, 0), (0, 0))).reshape(nb, blk, PEER_HEADS, PEER_TOPK)

    def one(args):
        hb, eb, gb = args
        ub = jnp.take(u_tab, eb, axis=0)
        vb = jnp.take(v_tab, eb, axis=0)
        a = jax.nn.gelu(jnp.einsum('nd,nhkd->nhk', hb, ub), approximate=False)
        return jnp.einsum('nhk,nhkd->nd', gb * a, vb)

    out = lax.map(one, (hp, ep, gp))
    return out.reshape(nb * blk, D)[:N]


def block(x, c, pool_hist, pos0, w_ada, b_ada, g1, w_in, pool_w, pool_scale,
          gmlp_g, gmlp_ws, gmlp_bs, w_out, g2, peer_wq, peer_keys, peer_u, peer_v):
    B, T, D = x.shape
    mod = (jax.nn.silu(c) @ w_ada + b_ada).reshape(B, N_MOD, D)
    shift1, scale1, gate1, shift2, scale2, gate2 = (mod[:, i] for i in range(N_MOD))
    h = modulate(rmsnorm(x, g1), shift1, scale1)
    z = h @ w_in
    p = z[..., :POOL_WIDTH]
    u = jax.nn.gelu(z[..., POOL_WIDTH:POOL_WIDTH + GMLP_WIDTH], approximate=False)
    v = jax.nn.gelu(z[..., POOL_WIDTH + GMLP_WIDTH:], approximate=False)
    pool_out = pool_mixer(p, pool_hist, pos0, pool_w, pool_scale)
    gm_out, vn = gmlp_mixer(u, v, gmlp_g, gmlp_ws, gmlp_bs)
    mix = jnp.concatenate([pool_out, gm_out], axis=-1) @ w_out
    x = x + gate1[:, None, :] * mix
    h2 = modulate(rmsnorm(x, g2), shift2, scale2).reshape(B * T, D)
    expert, gate = peer_retrieve(h2, peer_wq, peer_keys)
    ff = peer_apply(h2, expert, gate, peer_u, peer_v).reshape(B, T, D)
    x = x + gate2[:, None, :] * ff
    new_hist = jnp.concatenate([pool_hist, p], axis=1)[:, -POOL_HIST:]
    return x, new_hist, vn


def setup_inputs(seed: int = 0) -> dict:
    key = jax.random.key(seed)
    ks = jax.random.split(key, 24)
    f32 = jnp.float32
    nrm = lambda k, shape, s: (jax.random.normal(k, shape, f32) * s).astype(f32)
    D = D_MODEL
    return {
        "x_prompt": nrm(ks[0], (BATCH, SEQ, D), 1.0),
        "x_sample": nrm(ks[1], (DEC_BATCH, DEC_SEQ, D), 1.0),
        "cache_pool": nrm(ks[2], (DEPTH, DEC_BATCH, POOL_HIST, POOL_WIDTH), 1.0),
        "c_prompt": nrm(ks[3], (BATCH, D), 1.0),
        "c_sample": nrm(ks[4], (DEC_BATCH, D), 1.0),
        "w_ada": nrm(ks[5], (DEPTH, D, N_MOD * D), 0.5 * D ** -0.5),
        "b_ada": nrm(ks[6], (DEPTH, N_MOD * D), 0.01),
        "g_norm1": 1.0 + nrm(ks[7], (DEPTH, D), 0.02),
        "w_in": nrm(ks[8], (DEPTH, D, POOL_WIDTH + 2 * GMLP_WIDTH), D ** -0.5),
        "pool_w": nrm(ks[9], (DEPTH, POOL_GROUPS, POOL_GROUP_DIM, POOL_GROUP_DIM), POOL_GROUP_DIM ** -0.5),
        "pool_scale": 1.0 + nrm(ks[10], (DEPTH, POOL_WIDTH), 0.02),
        "gmlp_norm_g": 1.0 + nrm(ks[11], (DEPTH, GMLP_WIDTH), 0.02),
        "gmlp_ws": nrm(ks[12], (DEPTH, GMLP_HEADS, GMLP_CHUNK, GMLP_CHUNK), GMLP_CHUNK ** -0.5),
        "gmlp_bs": nrm(ks[13], (DEPTH, GMLP_HEADS, GMLP_CHUNK), 0.02),
        "w_out": nrm(ks[14], (DEPTH, MIX_WIDTH, D), MIX_WIDTH ** -0.5),
        "g_norm2": 1.0 + nrm(ks[15], (DEPTH, D), 0.02),
        "peer_wq": nrm(ks[16], (DEPTH, D, PEER_HEADS * PEER_DKEY), D ** -0.5),
        "peer_keys": nrm(ks[17], (DEPTH, PEER_HEADS, 2, PEER_NKEYS, PEER_DHALF), PEER_DHALF ** -0.5),
        "peer_u": nrm(ks[18], (DEPTH, PEER_EXPERTS, D), D ** -0.5),
        "peer_v": nrm(ks[19], (DEPTH, PEER_EXPERTS, D), PEER_HEADS ** -0.5),
        "g_final": 1.0 + nrm(ks[20], (D,), 0.02),
    }


def reference(x_prompt, x_sample, cache_pool, c_prompt, c_sample, w_ada, b_ada, g_norm1, w_in,
              pool_w, pool_scale, gmlp_norm_g, gmlp_ws, gmlp_bs, w_out, g_norm2, peer_wq,
              peer_keys, peer_u, peer_v, g_final):
    xp, xs = x_prompt, x_sample
    pool_prompt_new, pool_sample_new, gmlp_v_sample_new = [], [], []
    for l in range(DEPTH):
        weights = (w_ada[l], b_ada[l], g_norm1[l], w_in[l], pool_w[l], pool_scale[l],
                   gmlp_norm_g[l], gmlp_ws[l], gmlp_bs[l], w_out[l], g_norm2[l],
                   peer_wq[l], peer_keys[l], peer_u[l], peer_v[l])
        hist0 = jnp.zeros((xp.shape[0], POOL_HIST, POOL_WIDTH), xp.dtype)
        xp, hp, _ = block(xp, c_prompt, hist0, 0, *weights)
        xs, hs, vs = block(xs, c_sample, cache_pool[l], PAST_LEN, *weights)
        pool_prompt_new.append(hp)
        pool_sample_new.append(hs)
        gmlp_v_sample_new.append(vs)
    y_prompt = rmsnorm(xp, g_final)
    y_sample = rmsnorm(xs, g_final)
    return (y_prompt, y_sample, jnp.stack(pool_prompt_new), jnp.stack(pool_sample_new), jnp.stack(gmlp_v_sample_new))
```

```python
import functools

import jax
import jax.numpy as jnp
from jax import lax
from jax.experimental import pallas as pl
from jax.experimental.pallas import tpu as pltpu

EPS = 1e-6
POOL_WINDOWS = (2, 4, 8, 16)
POOL_GROUP_DIM = 128
POOL_HIST_PAD = 16
GMLP_HEADS = 4
GMLP_HEAD_DIM = 128
GMLP_CHUNK = 128
PEER_HEADS = 8
PEER_NKEYS = 128
PEER_TOPK = 16
PEER_DHALF = 128
N_MOD = 6

LANES = 128
SUBLANES = 8
VMEM_LIMIT_BYTES = 56 * 1024 * 1024

MIXER_TOKENS = 256
PEER_TOKENS = 512
PEER_EXPERTS_TILE = 1024

_BF16 = jnp.bfloat16
_F32 = jnp.float32


def _gelu(x):
    return 0.5 * x * (1.0 + lax.erf(x * 0.7071067811865476))


def _rms(x, g):
    return x * lax.rsqrt(jnp.mean(x * x, axis=-1, keepdims=True) + EPS) * g


def _dot(a, b):
    return jnp.dot(a, b, preferred_element_type=_F32)


def _ada_kernel(c_ref, w_ref, b_ref, o_ref):
    c = c_ref[...]
    sc = c * jax.nn.sigmoid(c)
    o_ref[...] = _dot(sc.astype(_BF16), w_ref[...].astype(_BF16)) + b_ref[...]


def _ada(c, w, b):
    rows, d = c.shape
    n = w.shape[1]
    tile = 1536
    return pl.pallas_call(
        _ada_kernel,
        grid=(n // tile,),
        in_specs=[pl.BlockSpec((rows, d), lambda j: (0, 0)),
                  pl.BlockSpec((d, tile), lambda j: (0, j)),
                  pl.BlockSpec((1, tile), lambda j: (0, j))],
        out_specs=pl.BlockSpec((rows, tile), lambda j: (0, j)),
        out_shape=jax.ShapeDtypeStruct((rows, n), _F32),
        compiler_params=pltpu.CompilerParams(
            dimension_semantics=("arbitrary",), vmem_limit_bytes=VMEM_LIMIT_BYTES),
        name="adaln",
    )(c, w, b.reshape(1, n))


def _top_desc(s, k, emit):
    cur = s
    for i in range(k):
        m = jnp.max(cur, axis=0, keepdims=True)
        emit(i, m)
        if i + 1 < k:
            cur = jnp.where(cur == m, -jnp.inf, cur)


def _mixer_kernel(x_ref, mod_ref, hist_ref, g1_ref, w_in_ref, pool_w_ref, pool_scale_ref,
                  gmlp_g_ref, ws_ref, bs_ref, w_out_ref, g2_ref, wqt_ref, keys_ref,
                  *rest, seqs, rows, chunk, pos0, carry, emit_vn):
    if emit_vn:
        (x1_ref, h2t_ref, s_ref, e_ref, thr_ref, hist_out_ref, vn_ref,
         pbuf, qt_scr, sv_scr) = rest
    else:
        (x1_ref, h2t_ref, s_ref, e_ref, thr_ref, hist_out_ref,
         pbuf, qt_scr, sv_scr) = rest
        vn_ref = None
    t = pl.program_id(1)
    tt = seqs * rows
    pool_width = pool_w_ref.shape[0] * POOL_GROUP_DIM
    gmlp_width = GMLP_HEADS * GMLP_HEAD_DIM

    x = x_ref[...]
    shift1, scale1, gate1, shift2, scale2 = (mod_ref[i] for i in range(5))
    h = _rms(x, g1_ref[...]) * (1.0 + scale1) + shift1
    z = _dot(h.astype(_BF16), w_in_ref[...])
    p = z[:, :pool_width]
    u = _gelu(z[:, pool_width:pool_width + gmlp_width])
    v = _gelu(z[:, pool_width + gmlp_width:])

    if carry:
        @pl.when(t == 0)
        def _():
            pbuf[:, 0:POOL_HIST_PAD, :] = hist_ref[...]
    else:
        pbuf[:, 0:POOL_HIST_PAD, :] = hist_ref[...]
    p3 = p.reshape(seqs, rows, pool_width)
    pbuf[:, POOL_HIST_PAD:POOL_HIST_PAD + rows, :] = p3
    pos = (pos0 + t * rows + lax.broadcasted_iota(jnp.int32, (1, rows, 1), 1)).astype(_F32)
    pool_parts = []
    for g, w in enumerate(POOL_WINDOWS):
        cols = slice(g * POOL_GROUP_DIM, (g + 1) * POOL_GROUP_DIM)
        tok = p3[:, :, cols]
        acc = tok
        for k in range(1, w):
            acc = acc + pbuf[:, POOL_HIST_PAD - k:POOL_HIST_PAD - k + rows, cols]
        cnt = jnp.minimum(pos + 1.0, float(w))
        diff = (acc / cnt - tok).reshape(tt, POOL_GROUP_DIM)
        pool_parts.append(_dot(diff.astype(_BF16), pool_w_ref[g]))
    pool_out = jnp.concatenate(pool_parts, axis=-1) * pool_scale_ref[...]
    hist_out_ref[...] = pbuf[:, rows:rows + POOL_HIST_PAD, :]
    if carry:
        pbuf[:, 0:POOL_HIST_PAD, :] = pbuf[:, rows:rows + POOL_HIST_PAD, :]

    cw = ws_ref.shape[1]
    ri = lax.broadcasted_iota(jnp.int32, (cw, cw), 0)
    ci = lax.broadcasted_iota(jnp.int32, (cw, cw), 1)
    causal = (ri // chunk == ci // chunk) & (ci % chunk <= ri % chunk)
    gm_parts = []
    vn_parts = []
    for hh in range(GMLP_HEADS):
        cols = slice(hh * GMLP_HEAD_DIM, (hh + 1) * GMLP_HEAD_DIM)
        vn = _rms(v[:, cols], gmlp_g_ref[:, cols])
        vn_parts.append(vn)
        wm = jnp.where(causal, ws_ref[hh], 0.0).astype(_BF16)
        vnb = vn.astype(_BF16)
        sp = [_dot(wm, vnb[c * cw:(c + 1) * cw]) + bs_ref[:, hh:hh + 1] for c in range(tt // cw)]
        s = sp[0] if len(sp) == 1 else jnp.concatenate(sp, axis=0)
        gm_parts.append(u[:, cols] * s)
    if vn_ref is not None:
        vn_ref[...] = jnp.concatenate(vn_parts, axis=-1)

    mix_in = jnp.concatenate([pool_out] + gm_parts, axis=-1)
    mix = _dot(mix_in.astype(_BF16), w_out_ref[...])
    x1 = x + gate1 * mix
    x1_ref[...] = x1

    h2 = _rms(x1, g2_ref[...]) * (1.0 + scale2) + shift2
    h2t = h2.T.astype(_BF16)
    h2t_ref[...] = h2t
    qt_scr[...] = _dot(wqt_ref[...], h2t)

    def score_body(hp, c):
        q = qt_scr[pl.ds(pl.multiple_of(hp * PEER_DHALF, PEER_DHALF), PEER_DHALF), :]
        s = _dot(keys_ref[hp], q.astype(_BF16))
        s_ref[hp] = s

        def keep(i, row):
            sv_scr[hp, i:i + 1, :] = row
        _top_desc(s, PEER_TOPK, keep)
        return c
    lax.fori_loop(0, 2 * PEER_HEADS, score_body, 0)

    row8 = lax.broadcasted_iota(jnp.int32, (SUBLANES, tt), 0)

    def head_body(hd, c):
        sv1 = sv_scr[2 * hd]
        sv2 = sv_scr[2 * hd + 1]
        cands = [sv1[0:1] + sv2]
        for a in range(1, PEER_TOPK):
            nb = PEER_TOPK // (a + 1)
            cands.append(jnp.where(row8 < nb, sv1[a:a + 1] + sv2[0:SUBLANES], -jnp.inf))
        cand = jnp.concatenate(cands, axis=0)
        last = []
        _top_desc(cand, PEER_TOPK, lambda i, row: last.append(row))
        thr = last[PEER_TOPK - 1]
        mx = sv1[0:1] + sv2[0:1]
        zsum = jnp.sum(jnp.where(cand >= thr, jnp.exp(cand - mx), 0.0), axis=0, keepdims=True)
        thr_ref[hd] = thr
        e_ref[2 * hd] = jnp.exp(s_ref[2 * hd] - sv1[0:1])
        e_ref[2 * hd + 1] = jnp.exp(s_ref[2 * hd + 1] - sv2[0:1]) / zsum
        return c
    lax.fori_loop(0, PEER_HEADS, head_body, 0)


def _mixer(x, mod, hist, g1, w_in, pool_w, pool_scale, gmlp_g, ws, bs, w_out, g2, wqt, keys,
           *, batch, seq, seqs, rows, chunk, pos0, emit_vn):
    n, d = x.shape
    tt = seqs * rows
    steps = seq // rows
    carry = steps > 1
    nb = batch // seqs
    pool_width = pool_w.shape[0] * POOL_GROUP_DIM
    gmlp_width = GMLP_HEADS * GMLP_HEAD_DIM
    hp2 = 2 * PEER_HEADS
    mod_rows = mod.shape[1]
    tok = lambda b, t: (b * steps + t, 0)
    const2 = lambda b, t: (0, 0)
    const3 = lambda b, t: (0, 0, 0)
    if mod_rows == 1:
        mod_spec = pl.BlockSpec((N_MOD, 1, d), lambda b, t: (b, 0, 0))
    else:
        mod_spec = pl.BlockSpec((N_MOD, tt, d), lambda b, t: (0, b * steps + t, 0))
    in_specs = [
        pl.BlockSpec((tt, d), tok),
        mod_spec,
        pl.BlockSpec((seqs, POOL_HIST_PAD, pool_width), lambda b, t: (b, 0, 0)),
        pl.BlockSpec((1, d), const2),
        pl.BlockSpec(w_in.shape, const2),
        pl.BlockSpec(pool_w.shape, const3),
        pl.BlockSpec((1, pool_width), const2),
        pl.BlockSpec((1, gmlp_width), const2),
        pl.BlockSpec(ws.shape, const3),
        pl.BlockSpec(bs.shape, const2),
        pl.BlockSpec(w_out.shape, const2),
        pl.BlockSpec((1, d), const2),
        pl.BlockSpec(wqt.shape, const2),
        pl.BlockSpec(keys.shape, const3),
    ]
    out_specs = [
        pl.BlockSpec((tt, d), tok),
        pl.BlockSpec((d, tt), lambda b, t: (0, b * steps + t)),
        pl.BlockSpec((hp2, PEER_NKEYS, tt), lambda b, t: (0, 0, b * steps + t)),
        pl.BlockSpec((hp2, PEER_NKEYS, tt), lambda b, t: (0, 0, b * steps + t)),
        pl.BlockSpec((PEER_HEADS, 1, tt), lambda b, t: (0, 0, b * steps + t)),
        pl.BlockSpec((seqs, POOL_HIST_PAD, pool_width), lambda b, t: (b, 0, 0)),
    ]
    out_shape = [
        jax.ShapeDtypeStruct((n, d), _F32),
        jax.ShapeDtypeStruct((d, n), _BF16),
        jax.ShapeDtypeStruct((hp2, PEER_NKEYS, n), _F32),
        jax.ShapeDtypeStruct((hp2, PEER_NKEYS, n), _F32),
        jax.ShapeDtypeStruct((PEER_HEADS, 1, n), _F32),
        jax.ShapeDtypeStruct((batch, POOL_HIST_PAD, pool_width), _F32),
    ]
    if emit_vn:
        out_specs.append(pl.BlockSpec((tt, gmlp_width), tok))
        out_shape.append(jax.ShapeDtypeStruct((n, gmlp_width), _F32))
    body = functools.partial(_mixer_kernel, seqs=seqs, rows=rows, chunk=chunk, pos0=pos0,
                             carry=carry, emit_vn=emit_vn)
    return pl.pallas_call(
        body,
        grid=(nb, steps),
        in_specs=in_specs,
        out_specs=out_specs,
        out_shape=out_shape,
        scratch_shapes=[
            pltpu.VMEM((seqs, POOL_HIST_PAD + rows, pool_width), _F32),
            pltpu.VMEM((hp2 * PEER_DHALF, tt), _F32),
            pltpu.VMEM((hp2, PEER_TOPK, tt), _F32),
        ],
        compiler_params=pltpu.CompilerParams(
            dimension_semantics=("arbitrary", "arbitrary"), vmem_limit_bytes=VMEM_LIMIT_BYTES),
        name="mixer_seq" if carry else "mixer_step",
    )(x, mod, hist, g1, w_in, pool_w, pool_scale, gmlp_g, ws, bs, w_out, g2, wqt, keys)


def _peer_kernel(h2t_ref, s1_ref, s2_ref, e1_ref, e2_ref, thr_ref, u_ref, vt_ref, x1_ref,
                 gate_ref, gf_ref, y_ref, a_scr, w_scr, acc_scr, *, final_norm):
    k = pl.program_id(1)
    te, tn = a_scr.shape
    n_i1 = te // PEER_NKEYS

    a_scr[...] = _dot(u_ref[...], h2t_ref[...])

    def lane_body(cc, c):
        lanes = pl.ds(pl.multiple_of(cc * LANES, LANES), LANES)
        for j in range(n_i1):
            rws = slice(j * PEER_NKEYS, (j + 1) * PEER_NKEYS)
            gate = jnp.zeros((PEER_NKEYS, LANES), _F32)
            for hd in range(PEER_HEADS):
                s1 = s1_ref[hd, j:j + 1, lanes]
                e1 = e1_ref[hd, j:j + 1, lanes]
                thr = thr_ref[hd, :, lanes]
                sel = (s2_ref[hd, :, lanes] + s1) >= thr
                gate = gate + jnp.where(sel, e2_ref[hd, :, lanes] * e1, 0.0)
            act = _gelu(a_scr[rws, lanes])
            w_scr[rws, lanes] = (gate * act).astype(_BF16)
        return c
    lax.fori_loop(0, tn // LANES, lane_body, 0)

    contrib = _dot(vt_ref[...], w_scr[...])

    @pl.when(k == 0)
    def _():
        acc_scr[...] = contrib

    @pl.when(k > 0)
    def _():
        acc_scr[...] += contrib

    @pl.when(k == pl.num_programs(1) - 1)
    def _():
        x = x1_ref[...] + gate_ref[...] * acc_scr[...].T
        y_ref[...] = _rms(x, gf_ref[...]) if final_norm else x


def _peer(h2t, s, e, thr, u_tab, vt_tab, x1, gate2, g_final, *, tn, tokens_per_gate, final_norm):
    d, n = h2t.shape
    n_exp = u_tab.shape[0]
    te = PEER_EXPERTS_TILE
    n_i1 = te // PEER_NKEYS
    s4 = s.reshape(PEER_HEADS, 2, PEER_NKEYS, n)
    e4 = e.reshape(PEER_HEADS, 2, PEER_NKEYS, n)
    gate_rows = gate2.shape[0]
    if gate_rows == n:
        gate_spec = pl.BlockSpec((tn, d), lambda i, k: (i, 0))
    else:
        gate_spec = pl.BlockSpec((None, 1, d), lambda i, k: ((i * tn) // tokens_per_gate, 0, 0))
        gate2 = gate2.reshape(gate_rows, 1, d)
    first = lambda i, k: (0, 0, k, i)
    second = lambda i, k: (0, 1, 0, i)
    return pl.pallas_call(
        functools.partial(_peer_kernel, final_norm=final_norm),
        grid=(n // tn, n_exp // te),
        in_specs=[
            pl.BlockSpec((d, tn), lambda i, k: (0, i)),
            pl.BlockSpec((PEER_HEADS, None, n_i1, tn), first),
            pl.BlockSpec((PEER_HEADS, None, PEER_NKEYS, tn), second),
            pl.BlockSpec((PEER_HEADS, None, n_i1, tn), first),
            pl.BlockSpec((PEER_HEADS, None, PEER_NKEYS, tn), second),
            pl.BlockSpec((PEER_HEADS, 1, tn), lambda i, k: (0, 0, i)),
            pl.BlockSpec((te, d), lambda i, k: (k, 0)),
            pl.BlockSpec((d, te), lambda i, k: (0, k)),
            pl.BlockSpec((tn, d), lambda i, k: (i, 0)),
            gate_spec,
            pl.BlockSpec((1, d), lambda i, k: (0, 0)),
        ],
        out_specs=pl.BlockSpec((tn, d), lambda i, k: (i, 0)),
        out_shape=jax.ShapeDtypeStruct((n, d), _F32),
        scratch_shapes=[
            pltpu.VMEM((te, tn), _F32),
            pltpu.VMEM((te, tn), _BF16),
            pltpu.VMEM((d, tn), _F32),
        ],
        compiler_params=pltpu.CompilerParams(
            dimension_semantics=("arbitrary", "arbitrary"), vmem_limit_bytes=VMEM_LIMIT_BYTES),
        name="peer_prompt" if tn == PEER_TOKENS else "peer_step",
    )(h2t, s4, s4, e4, e4, thr, u_tab, vt_tab, x1, gate2, g_final.reshape(1, d))


def kernel(x_prompt, x_sample, cache_pool, c_prompt, c_sample, w_ada, b_ada, g_norm1, w_in,
           pool_w, pool_scale, gmlp_norm_g, gmlp_ws, gmlp_bs, w_out, g_norm2, peer_wq,
           peer_keys, peer_u, peer_v, g_final):
    depth = w_ada.shape[0]
    batch, seq, d = x_prompt.shape
    dec_batch, dec_seq, _ = x_sample.shape
    past_len = 2048
    n_p, n_s = batch * seq, dec_batch * dec_seq
    pool_width = cache_pool.shape[-1]
    assert seq % MIXER_TOKENS == 0 and MIXER_TOKENS % GMLP_CHUNK == 0
    assert dec_seq % SUBLANES == 0 and dec_seq <= GMLP_CHUNK and POOL_HIST_PAD <= dec_seq
    assert n_p % PEER_TOKENS == 0 and seq % PEER_TOKENS == 0

    xp = x_prompt.reshape(n_p, d)
    xs = x_sample.reshape(n_s, d)
    c_rows = batch + dec_batch
    c_pad = -(-c_rows // SUBLANES) * SUBLANES
    c_all = jnp.concatenate([c_prompt, c_sample, jnp.zeros((c_pad - c_rows, d), _F32)], axis=0)
    reps = n_s // dec_seq

    pool_p, pool_s, vn_s = [], [], []
    for l in range(depth):
        mod = _ada(c_all, w_ada[l], b_ada[l]).reshape(c_pad, N_MOD, d)
        mod_p = mod[:batch].reshape(batch * N_MOD, 1, d)
        mod_s = jnp.repeat(mod[batch:c_rows].transpose(1, 0, 2), dec_seq, axis=1)
        gate2_p = mod[:batch, 5]
        gate2_s = mod_s[5]

        w_in_b = w_in[l].astype(_BF16)
        pool_w_b = pool_w[l].astype(_BF16)
        w_out_b = w_out[l].astype(_BF16)
        wqt_b = peer_wq[l].T.astype(_BF16)
        keys_b = peer_keys[l].reshape(2 * PEER_HEADS, PEER_NKEYS, PEER_DHALF).astype(_BF16)
        u_b = peer_u[l].astype(_BF16)
        vt_b = peer_v[l].T.astype(_BF16)
        g1 = g_norm1[l].reshape(1, d)
        g2 = g_norm2[l].reshape(1, d)
        pscale = pool_scale[l].reshape(1, pool_width)
        gg = gmlp_norm_g[l].reshape(1, -1)
        final = l == depth - 1

        common = (g1, w_in_b, pool_w_b, pscale, gg)
        tail = (w_out_b, g2, wqt_b, keys_b)

        hist0 = jnp.zeros((batch, POOL_HIST_PAD, pool_width), _F32)
        x1, h2t, s, e, thr, hist_p = _mixer(
            xp, mod_p, hist0, *common, gmlp_ws[l], gmlp_bs[l].T, *tail,
            batch=batch, seq=seq, seqs=1, rows=MIXER_TOKENS, chunk=GMLP_CHUNK, pos0=0,
            emit_vn=False)
        xp = _peer(h2t, s, e, thr, u_b, vt_b, x1, gate2_p, g_final, tn=PEER_TOKENS,
                   tokens_per_gate=seq, final_norm=final)

        hist_s = jnp.pad(cache_pool[l], ((0, 0), (POOL_HIST_PAD - cache_pool.shape[2], 0), (0, 0)))
        ws_s = jnp.tile(gmlp_ws[l][:, :dec_seq, :dec_seq], (1, reps, reps))
        bs_s = jnp.tile(gmlp_bs[l][:, :dec_seq], (1, reps)).T
        x1, h2t, s, e, thr, hist_sn, vn = _mixer(
            xs, mod_s, hist_s, *common, ws_s, bs_s, *tail,
            batch=dec_batch, seq=dec_seq, seqs=dec_batch, rows=dec_seq, chunk=dec_seq,
            pos0=past_len, emit_vn=True)
        xs = _peer(h2t, s, e, thr, u_b, vt_b, x1, gate2_s, g_final, tn=n_s,
                   tokens_per_gate=1, final_norm=final)

        pool_p.append(hist_p[:, 1:])
        pool_s.append(hist_sn[:, 1:])
        vn_s.append(vn.reshape(dec_batch, dec_seq, -1))

    return (xp.reshape(batch, seq, d), xs.reshape(dec_batch, dec_seq, d),
            jnp.stack(pool_p), jnp.stack(pool_s), jnp.stack(vn_s))
```

```python
import functools

import jax
import jax.numpy as jnp
from jax import lax
from jax.experimental import pallas as pl
from jax.experimental.pallas import tpu as pltpu

EPS = 1e-6
POOL_WINDOWS = (2, 4, 8, 16)
POOL_GROUP_DIM = 128
POOL_HIST_PAD = 16
GMLP_HEADS = 4
GMLP_HEAD_DIM = 128
GMLP_CHUNK = 128
PEER_HEADS = 8
PEER_NKEYS = 128
PEER_TOPK = 16
PEER_DHALF = 128
N_MOD = 6

LANES = 128
SUBLANES = 8
VMEM_LIMIT_BYTES = 56 * 1024 * 1024

MIXER_TOKENS = 256
PEER_TOKENS = 512
PEER_EXPERTS_TILE = 1024

_BF16 = jnp.bfloat16
_F32 = jnp.float32


def _gelu(x):
    return 0.5 * x * (1.0 + lax.erf(x * 0.7071067811865476))


def _rms(x, g):
    return x * lax.rsqrt(jnp.mean(x * x, axis=-1, keepdims=True) + EPS) * g


def _dot(a, b):
    return jnp.dot(a, b, preferred_element_type=_F32)


def _ada_kernel(c_ref, w_ref, b_ref, o_ref):
    c = c_ref[...]
    sc = c * jax.nn.sigmoid(c)
    o_ref[...] = _dot(sc.astype(_BF16), w_ref[...].astype(_BF16)) + b_ref[...]


def _ada(c, w, b):
    rows, d = c.shape
    n = w.shape[1]
    tile = 1536
    return pl.pallas_call(
        _ada_kernel,
        grid=(n // tile,),
        in_specs=[pl.BlockSpec((rows, d), lambda j: (0, 0)),
                  pl.BlockSpec((d, tile), lambda j: (0, j)),
                  pl.BlockSpec((1, tile), lambda j: (0, j))],
        out_specs=pl.BlockSpec((rows, tile), lambda j: (0, j)),
        out_shape=jax.ShapeDtypeStruct((rows, n), _F32),
        compiler_params=pltpu.CompilerParams(
            dimension_semantics=("arbitrary",), vmem_limit_bytes=VMEM_LIMIT_BYTES),
        name="adaln",
    )(c, w, b.reshape(1, n))


def _top_desc(s, k, emit):
    cur = s
    for i in range(k):
        m = jnp.max(cur, axis=0, keepdims=True)
        emit(i, m)
        if i + 1 < k:
            cur = jnp.where(cur == m, -jnp.inf, cur)


def _mixer_kernel(x_ref, mod_ref, hist_ref, g1_ref, w_in_ref, pool_w_ref, pool_scale_ref,
                  gmlp_g_ref, ws_ref, bs_ref, w_out_ref, g2_ref, wqt_ref, keys_ref,
                  *rest, seqs, rows, chunk, pos0, carry, emit_vn):
    if emit_vn:
        (x1_ref, h2t_ref, s2_ref, thr2_ref, e1_ref, e2_ref, hist_out_ref, vn_ref,
         pbuf, qt_scr, s_scr, sv_scr) = rest
    else:
        (x1_ref, h2t_ref, s2_ref, thr2_ref, e1_ref, e2_ref, hist_out_ref,
         pbuf, qt_scr, s_scr, sv_scr) = rest
        vn_ref = None
    t = pl.program_id(1)
    tt = seqs * rows
    pool_width = pool_w_ref.shape[0] * POOL_GROUP_DIM
    gmlp_width = GMLP_HEADS * GMLP_HEAD_DIM

    x = x_ref[...]
    shift1, scale1, gate1, shift2, scale2 = (mod_ref[i] for i in range(5))
    h = _rms(x, g1_ref[...]) * (1.0 + scale1) + shift1
    z = _dot(h.astype(_BF16), w_in_ref[...])
    p = z[:, :pool_width]
    u = _gelu(z[:, pool_width:pool_width + gmlp_width])
    v = _gelu(z[:, pool_width + gmlp_width:])

    if carry:
        @pl.when(t == 0)
        def _():
            pbuf[:, 0:POOL_HIST_PAD, :] = hist_ref[...]
    else:
        pbuf[:, 0:POOL_HIST_PAD, :] = hist_ref[...]
    p3 = p.reshape(seqs, rows, pool_width)
    pbuf[:, POOL_HIST_PAD:POOL_HIST_PAD + rows, :] = p3
    pos = (pos0 + t * rows + lax.broadcasted_iota(jnp.int32, (1, rows, 1), 1)).astype(_F32)
    pool_parts = []
    for g, w in enumerate(POOL_WINDOWS):
        cols = slice(g * POOL_GROUP_DIM, (g + 1) * POOL_GROUP_DIM)
        tok = p3[:, :, cols]
        acc = tok
        for k in range(1, w):
            acc = acc + pbuf[:, POOL_HIST_PAD - k:POOL_HIST_PAD - k + rows, cols]
        cnt = jnp.minimum(pos + 1.0, float(w))
        diff = (acc / cnt - tok).reshape(tt, POOL_GROUP_DIM)
        pool_parts.append(_dot(diff.astype(_BF16), pool_w_ref[g]))
    pool_out = jnp.concatenate(pool_parts, axis=-1) * pool_scale_ref[...]
    hist_out_ref[...] = pbuf[:, rows:rows + POOL_HIST_PAD, :]
    if carry:
        pbuf[:, 0:POOL_HIST_PAD, :] = pbuf[:, rows:rows + POOL_HIST_PAD, :]

    cw = ws_ref.shape[1]
    ri = lax.broadcasted_iota(jnp.int32, (cw, cw), 0)
    ci = lax.broadcasted_iota(jnp.int32, (cw, cw), 1)
    causal = (ri // chunk == ci // chunk) & (ci % chunk <= ri % chunk)
    gm_parts = []
    vn_parts = []
    for hh in range(GMLP_HEADS):
        cols = slice(hh * GMLP_HEAD_DIM, (hh + 1) * GMLP_HEAD_DIM)
        vn = _rms(v[:, cols], gmlp_g_ref[:, cols])
        vn_parts.append(vn)
        wm = jnp.where(causal, ws_ref[hh], 0.0).astype(_BF16)
        vnb = vn.astype(_BF16)
        sp = [_dot(wm, vnb[c * cw:(c + 1) * cw]) + bs_ref[:, hh:hh + 1] for c in range(tt // cw)]
        s = sp[0] if len(sp) == 1 else jnp.concatenate(sp, axis=0)
        gm_parts.append(u[:, cols] * s)
    if vn_ref is not None:
        vn_ref[...] = jnp.concatenate(vn_parts, axis=-1)

    mix_in = jnp.concatenate([pool_out] + gm_parts, axis=-1)
    mix = _dot(mix_in.astype(_BF16), w_out_ref[...])
    x1 = x + gate1 * mix
    x1_ref[...] = x1

    h2 = _rms(x1, g2_ref[...]) * (1.0 + scale2) + shift2
    h2t = h2.T.astype(_BF16)
    h2t_ref[...] = h2t
    qt_scr[...] = _dot(wqt_ref[...], h2t)

    def score_body(hp, c):
        q = qt_scr[pl.ds(pl.multiple_of(hp * PEER_DHALF, PEER_DHALF), PEER_DHALF), :]
        s = _dot(keys_ref[hp], q.astype(_BF16))
        s_scr[hp] = s

        def keep(i, row):
            sv_scr[hp, i:i + 1, :] = row
        _top_desc(s, PEER_TOPK, keep)
        return c
    lax.fori_loop(0, 2 * PEER_HEADS, score_body, 0)

    row8 = lax.broadcasted_iota(jnp.int32, (SUBLANES, tt), 0)
    row16 = lax.broadcasted_iota(jnp.int32, (PEER_TOPK, tt), 0)

    def by_chunk(ref, hd, val):
        for cc in range(tt // LANES):
            ref[hd, cc] = val[:, cc * LANES:(cc + 1) * LANES]

    def head_body(hd, c):
        sv1 = sv_scr[2 * hd]
        sv2 = sv_scr[2 * hd + 1]
        cands = [sv1[0:1] + sv2]
        for a in range(1, PEER_TOPK):
            nb = PEER_TOPK // (a + 1)
            cands.append(jnp.where(row8 < nb, sv1[a:a + 1] + sv2[0:SUBLANES], -jnp.inf))
        cand = jnp.concatenate(cands, axis=0)
        last = []
        _top_desc(cand, PEER_TOPK, lambda i, row: last.append(row))
        thr = last[PEER_TOPK - 1]
        mx = sv1[0:1] + sv2[0:1]
        zsum = jnp.sum(jnp.where(cand >= thr, jnp.exp(cand - mx), 0.0), axis=0, keepdims=True)

        s1 = s_scr[2 * hd]
        s2 = s_scr[2 * hd + 1]
        thr2 = jnp.full(s1.shape, jnp.inf, _F32)
        for a in range(PEER_TOPK):
            nb = PEER_TOPK // (a + 1)
            part, rws = (sv2, row16) if nb > SUBLANES else (sv2[0:SUBLANES], row8)
            ok = ((sv1[a:a + 1] + part) >= thr) & (rws < nb)
            least = jnp.min(jnp.where(ok, part, jnp.inf), axis=0, keepdims=True)
            thr2 = jnp.where(s1 == sv1[a:a + 1], least, thr2)
        by_chunk(thr2_ref, hd, thr2)
        by_chunk(s2_ref, hd, s2)
        by_chunk(e1_ref, hd, jnp.exp(s1 - sv1[0:1]))
        by_chunk(e2_ref, hd, jnp.exp(s2 - sv2[0:1]) / zsum)
        return c
    lax.fori_loop(0, PEER_HEADS, head_body, 0)


def _mixer(x, mod, hist, g1, w_in, pool_w, pool_scale, gmlp_g, ws, bs, w_out, g2, wqt, keys,
           *, batch, seq, seqs, rows, chunk, pos0, emit_vn):
    n, d = x.shape
    tt = seqs * rows
    steps = seq // rows
    carry = steps > 1
    nb = batch // seqs
    pool_width = pool_w.shape[0] * POOL_GROUP_DIM
    gmlp_width = GMLP_HEADS * GMLP_HEAD_DIM
    hp2 = 2 * PEER_HEADS
    mod_rows = mod.shape[1]
    tok = lambda b, t: (b * steps + t, 0)
    const2 = lambda b, t: (0, 0)
    const3 = lambda b, t: (0, 0, 0)
    if mod_rows == 1:
        mod_spec = pl.BlockSpec((N_MOD, 1, d), lambda b, t: (b, 0, 0))
    else:
        mod_spec = pl.BlockSpec((N_MOD, tt, d), lambda b, t: (0, b * steps + t, 0))
    in_specs = [
        pl.BlockSpec((tt, d), tok),
        mod_spec,
        pl.BlockSpec((seqs, POOL_HIST_PAD, pool_width), lambda b, t: (b, 0, 0)),
        pl.BlockSpec((1, d), const2),
        pl.BlockSpec(w_in.shape, const2),
        pl.BlockSpec(pool_w.shape, const3),
        pl.BlockSpec((1, pool_width), const2),
        pl.BlockSpec((1, gmlp_width), const2),
        pl.BlockSpec(ws.shape, const3),
        pl.BlockSpec(bs.shape, const2),
        pl.BlockSpec(w_out.shape, const2),
        pl.BlockSpec((1, d), const2),
        pl.BlockSpec(wqt.shape, const2),
        pl.BlockSpec(keys.shape, const3),
    ]
    by_chunk_spec = pl.BlockSpec((PEER_HEADS, tt // LANES, PEER_NKEYS, LANES),
                                 lambda b, t: (0, b * steps + t, 0, 0))
    by_chunk_shape = jax.ShapeDtypeStruct((PEER_HEADS, n // LANES, PEER_NKEYS, LANES), _F32)
    out_specs = [
        pl.BlockSpec((tt, d), tok),
        pl.BlockSpec((d, tt), lambda b, t: (0, b * steps + t)),
        by_chunk_spec, by_chunk_spec, by_chunk_spec, by_chunk_spec,
        pl.BlockSpec((seqs, POOL_HIST_PAD, pool_width), lambda b, t: (b, 0, 0)),
    ]
    out_shape = [
        jax.ShapeDtypeStruct((n, d), _F32),
        jax.ShapeDtypeStruct((d, n), _BF16),
        by_chunk_shape, by_chunk_shape, by_chunk_shape, by_chunk_shape,
        jax.ShapeDtypeStruct((batch, POOL_HIST_PAD, pool_width), _F32),
    ]
    if emit_vn:
        out_specs.append(pl.BlockSpec((tt, gmlp_width), tok))
        out_shape.append(jax.ShapeDtypeStruct((n, gmlp_width), _F32))
    body = functools.partial(_mixer_kernel, seqs=seqs, rows=rows, chunk=chunk, pos0=pos0,
                             carry=carry, emit_vn=emit_vn)
    return pl.pallas_call(
        body,
        grid=(nb, steps),
        in_specs=in_specs,
        out_specs=out_specs,
        out_shape=out_shape,
        scratch_shapes=[
            pltpu.VMEM((seqs, POOL_HIST_PAD + rows, pool_width), _F32),
            pltpu.VMEM((hp2 * PEER_DHALF, tt), _F32),
            pltpu.VMEM((hp2, PEER_NKEYS, tt), _F32),
            pltpu.VMEM((hp2, PEER_TOPK, tt), _F32),
        ],
        compiler_params=pltpu.CompilerParams(
            dimension_semantics=("arbitrary", "arbitrary"), vmem_limit_bytes=VMEM_LIMIT_BYTES),
        name="mixer_seq" if carry else "mixer_step",
    )(x, mod, hist, g1, w_in, pool_w, pool_scale, gmlp_g, ws, bs, w_out, g2, wqt, keys)


def _peer_kernel(h2t_ref, s2_ref, thr2_ref, e1_ref, e2_ref, u_ref, vt_ref, x1_ref,
                 gate_ref, gf_ref, y_ref, a_scr, w_scr, acc_scr, *, final_norm):
    k = pl.program_id(1)
    te, tn = a_scr.shape
    n_i1 = te // PEER_NKEYS

    a_scr[...] = _dot(u_ref[...], h2t_ref[...])

    def lane_body(cc, c):
        lanes = pl.ds(pl.multiple_of(cc * LANES, LANES), LANES)
        for j in range(n_i1):
            rws = slice(j * PEER_NKEYS, (j + 1) * PEER_NKEYS)
            gate = jnp.zeros((PEER_NKEYS, LANES), _F32)
            for hd in range(PEER_HEADS):
                sel = s2_ref[hd, cc] >= thr2_ref[hd, cc, j:j + 1, :]
                gate = gate + jnp.where(sel, e2_ref[hd, cc] * e1_ref[hd, cc, j:j + 1, :], 0.0)
            act = _gelu(a_scr[rws, lanes])
            w_scr[rws, lanes] = (gate * act).astype(_BF16)
        return c
    lax.fori_loop(0, tn // LANES, lane_body, 0)

    contrib = _dot(vt_ref[...], w_scr[...])

    @pl.when(k == 0)
    def _():
        acc_scr[...] = contrib

    @pl.when(k > 0)
    def _():
        acc_scr[...] += contrib

    @pl.when(k == pl.num_programs(1) - 1)
    def _():
        x = x1_ref[...] + gate_ref[...] * acc_scr[...].T
        y_ref[...] = _rms(x, gf_ref[...]) if final_norm else x


def _peer(h2t, s2, thr2, e1, e2, u_tab, vt_tab, x1, gate2, g_final, *, tn, tokens_per_gate,
          final_norm):
    d, n = h2t.shape
    n_exp = u_tab.shape[0]
    te = PEER_EXPERTS_TILE
    n_i1 = te // PEER_NKEYS
    gate_rows = gate2.shape[0]
    if gate_rows == n:
        gate_spec = pl.BlockSpec((tn, d), lambda i, k: (i, 0))
    else:
        gate_spec = pl.BlockSpec((None, 1, d), lambda i, k: ((i * tn) // tokens_per_gate, 0, 0))
        gate2 = gate2.reshape(gate_rows, 1, d)
    per_i1 = pl.BlockSpec((PEER_HEADS, tn // LANES, n_i1, LANES), lambda i, k: (0, i, k, 0))
    per_i2 = pl.BlockSpec((PEER_HEADS, tn // LANES, PEER_NKEYS, LANES), lambda i, k: (0, i, 0, 0))
    return pl.pallas_call(
        functools.partial(_peer_kernel, final_norm=final_norm),
        grid=(n // tn, n_exp // te),
        in_specs=[
            pl.BlockSpec((d, tn), lambda i, k: (0, i)),
            per_i2, per_i1, per_i1, per_i2,
            pl.BlockSpec((te, d), lambda i, k: (k, 0)),
            pl.BlockSpec((d, te), lambda i, k: (0, k)),
            pl.BlockSpec((tn, d), lambda i, k: (i, 0)),
            gate_spec,
            pl.BlockSpec((1, d), lambda i, k: (0, 0)),
        ],
        out_specs=pl.BlockSpec((tn, d), lambda i, k: (i, 0)),
        out_shape=jax.ShapeDtypeStruct((n, d), _F32),
        scratch_shapes=[
            pltpu.VMEM((te, tn), _F32),
            pltpu.VMEM((te, tn), _BF16),
            pltpu.VMEM((d, tn), _F32),
        ],
        compiler_params=pltpu.CompilerParams(
            dimension_semantics=("arbitrary", "arbitrary"), vmem_limit_bytes=VMEM_LIMIT_BYTES),
        name="peer_prompt" if tn == PEER_TOKENS else "peer_step",
    )(h2t, s2, thr2, e1, e2, u_tab, vt_tab, x1, gate2, g_final.reshape(1, d))


def kernel(x_prompt, x_sample, cache_pool, c_prompt, c_sample, w_ada, b_ada, g_norm1, w_in,
           pool_w, pool_scale, gmlp_norm_g, gmlp_ws, gmlp_bs, w_out, g_norm2, peer_wq,
           peer_keys, peer_u, peer_v, g_final):
    depth = w_ada.shape[0]
    batch, seq, d = x_prompt.shape
    dec_batch, dec_seq, _ = x_sample.shape
    past_len = 2048
    n_p, n_s = batch * seq, dec_batch * dec_seq
    pool_width = cache_pool.shape[-1]
    assert seq % MIXER_TOKENS == 0 and MIXER_TOKENS % GMLP_CHUNK == 0
    assert dec_seq % SUBLANES == 0 and dec_seq <= GMLP_CHUNK and POOL_HIST_PAD <= dec_seq
    assert n_p % PEER_TOKENS == 0 and seq % PEER_TOKENS == 0

    xp = x_prompt.reshape(n_p, d)
    xs = x_sample.reshape(n_s, d)
    c_rows = batch + dec_batch
    c_pad = -(-c_rows // SUBLANES) * SUBLANES
    c_all = jnp.concatenate([c_prompt, c_sample, jnp.zeros((c_pad - c_rows, d), _F32)], axis=0)
    reps = n_s // dec_seq

    pool_p, pool_s, vn_s = [], [], []
    for l in range(depth):
        mod = _ada(c_all, w_ada[l], b_ada[l]).reshape(c_pad, N_MOD, d)
        mod_p = mod[:batch].reshape(batch * N_MOD, 1, d)
        mod_s = jnp.repeat(mod[batch:c_rows].transpose(1, 0, 2), dec_seq, axis=1)
        gate2_p = mod[:batch, 5]
        gate2_s = mod_s[5]

        w_in_b = w_in[l].astype(_BF16)
        pool_w_b = pool_w[l].astype(_BF16)
        w_out_b = w_out[l].astype(_BF16)
        wqt_b = peer_wq[l].T.astype(_BF16)
        keys_b = peer_keys[l].reshape(2 * PEER_HEADS, PEER_NKEYS, PEER_DHALF).astype(_BF16)
        u_b = peer_u[l].astype(_BF16)
        vt_b = peer_v[l].T.astype(_BF16)
        g1 = g_norm1[l].reshape(1, d)
        g2 = g_norm2[l].reshape(1, d)
        pscale = pool_scale[l].reshape(1, pool_width)
        gg = gmlp_norm_g[l].reshape(1, -1)
        final = l == depth - 1

        common = (g1, w_in_b, pool_w_b, pscale, gg)
        tail = (w_out_b, g2, wqt_b, keys_b)

        hist0 = jnp.zeros((batch, POOL_HIST_PAD, pool_width), _F32)
        x1, h2t, s2, thr2, e1, e2, hist_p = _mixer(
            xp, mod_p, hist0, *common, gmlp_ws[l], gmlp_bs[l].T, *tail,
            batch=batch, seq=seq, seqs=1, rows=MIXER_TOKENS, chunk=GMLP_CHUNK, pos0=0,
            emit_vn=False)
        xp = _peer(h2t, s2, thr2, e1, e2, u_b, vt_b, x1, gate2_p, g_final, tn=PEER_TOKENS,
                   tokens_per_gate=seq, final_norm=final)

        hist_s = jnp.pad(cache_pool[l], ((0, 0), (POOL_HIST_PAD - cache_pool.shape[2], 0), (0, 0)))
        ws_s = jnp.tile(gmlp_ws[l][:, :dec_seq, :dec_seq], (1, reps, reps))
        bs_s = jnp.tile(gmlp_bs[l][:, :dec_seq], (1, reps)).T
        x1, h2t, s2, thr2, e1, e2, hist_sn, vn = _mixer(
            xs, mod_s, hist_s, *common, ws_s, bs_s, *tail,
            batch=dec_batch, seq=dec_seq, seqs=dec_batch, rows=dec_seq, chunk=dec_seq,
            pos0=past_len, emit_vn=True)
        xs = _peer(h2t, s2, thr2, e1, e2, u_b, vt_b, x1, gate2_s, g_final, tn=n_s,
                   tokens_per_gate=1, final_norm=final)

        pool_p.append(hist_p[:, 1:])
        pool_s.append(hist_sn[:, 1:])
        vn_s.append(vn.reshape(dec_batch, dec_seq, -1))

    return (xp.reshape(batch, seq, d), xs.reshape(dec_batch, dec_seq, d),
            jnp.stack(pool_p), jnp.stack(pool_s), jnp.stack(vn_s))
```

```python
import functools

import jax
import jax.numpy as jnp
from jax import lax
from jax.experimental import pallas as pl
from jax.experimental.pallas import tpu as pltpu

EPS = 1e-6
POOL_WINDOWS = (2, 4, 8, 16)
POOL_GROUP_DIM = 128
POOL_HIST_PAD = 16
GMLP_HEADS = 4
GMLP_HEAD_DIM = 128
GMLP_CHUNK = 128
PEER_HEADS = 8
PEER_NKEYS = 128
PEER_TOPK = 16
PEER_DHALF = 128
UNRANKED = 127.0
N_MOD = 6

LANES = 128
SUBLANES = 8
BF16_ROWS = 16
VMEM_LIMIT_BYTES = 56 * 1024 * 1024

MIXER_TOKENS = 256
PEER_TOKENS = 512
PEER_EXPERTS_TILE = 1024
MXU_ROWS = 128

_BF16 = jnp.bfloat16
_F32 = jnp.float32


def _gelu(x):
    return 0.5 * x * (1.0 + lax.erf(x * 0.7071067811865476))


def _rms(x, g):
    return x * lax.rsqrt(jnp.mean(x * x, axis=-1, keepdims=True) + EPS) * g


def _dot(a, b):
    return jnp.dot(a, b, preferred_element_type=_F32)


def _pack_pairs(x):
    return pltpu.bitcast(x.astype(_BF16), jnp.uint32)


def _unpack_pairs(x):
    return pltpu.bitcast(x, _BF16)


def _ada_kernel(c_ref, w_ref, b_ref, o_ref):
    c = c_ref[...]
    sc = c * jax.nn.sigmoid(c)
    o_ref[...] = _dot(sc.astype(_BF16), w_ref[...].astype(_BF16)) + b_ref[...]


def _ada(c, w, b):
    rows, d = c.shape
    n = w.shape[1]
    tile = 1536
    return pl.pallas_call(
        _ada_kernel,
        grid=(n // tile,),
        in_specs=[pl.BlockSpec((rows, d), lambda j: (0, 0)),
                  pl.BlockSpec((d, tile), lambda j: (0, j)),
                  pl.BlockSpec((1, tile), lambda j: (0, j))],
        out_specs=pl.BlockSpec((rows, tile), lambda j: (0, j)),
        out_shape=jax.ShapeDtypeStruct((rows, n), _F32),
        compiler_params=pltpu.CompilerParams(
            dimension_semantics=("arbitrary",), vmem_limit_bytes=VMEM_LIMIT_BYTES),
        name="adaln",
    )(c, w, b.reshape(1, n))


def _top_desc(s, k, emit, with_rank=False):
    cur = s
    rank = jnp.full(s.shape, UNRANKED, _F32) if with_rank else None
    for i in range(k):
        m = jnp.max(cur, axis=0, keepdims=True)
        emit(i, m)
        if i + 1 < k or with_rank:
            hit = cur == m
            if with_rank:
                rank = jnp.where(hit, float(i), rank)
            cur = jnp.where(hit, -jnp.inf, cur)
    return rank


def _mixer_kernel(x_ref, mod_ref, hist_ref, g1_ref, w_in_ref, pool_w_ref, pool_scale_ref,
                  gmlp_g_ref, ws_ref, bs_ref, w_out_ref, g2_ref, wqt_ref, keys_ref,
                  *rest, seqs, rows, chunk, pos0, carry, emit_vn):
    if emit_vn:
        (x1_ref, h2t_ref, r2_ref, cnt_ref, e1_ref, e2_ref, hist_out_ref, vn_ref,
         pbuf, qt_scr, s_scr, r_scr, sv_scr) = rest
    else:
        (x1_ref, h2t_ref, r2_ref, cnt_ref, e1_ref, e2_ref, hist_out_ref,
         pbuf, qt_scr, s_scr, r_scr, sv_scr) = rest
        vn_ref = None
    t = pl.program_id(1)
    tt = seqs * rows
    pool_width = pool_w_ref.shape[0] * POOL_GROUP_DIM
    gmlp_width = GMLP_HEADS * GMLP_HEAD_DIM

    x = x_ref[...]
    shift1, scale1, gate1, shift2, scale2 = (mod_ref[i] for i in range(5))
    h = _rms(x, g1_ref[...]) * (1.0 + scale1) + shift1
    z = _dot(h.astype(_BF16), w_in_ref[...])
    p = z[:, :pool_width]
    u = _gelu(z[:, pool_width:pool_width + gmlp_width])
    v = _gelu(z[:, pool_width + gmlp_width:])

    if carry:
        @pl.when(t == 0)
        def _():
            pbuf[:, 0:POOL_HIST_PAD, :] = hist_ref[...]
    else:
        pbuf[:, 0:POOL_HIST_PAD, :] = hist_ref[...]
    p3 = p.reshape(seqs, rows, pool_width)
    pbuf[:, POOL_HIST_PAD:POOL_HIST_PAD + rows, :] = p3
    pos = (pos0 + t * rows + lax.broadcasted_iota(jnp.int32, (1, rows, 1), 1)).astype(_F32)
    pool_parts = []
    for g, w in enumerate(POOL_WINDOWS):
        cols = slice(g * POOL_GROUP_DIM, (g + 1) * POOL_GROUP_DIM)
        tok = p3[:, :, cols]
        acc = tok
        for k in range(1, w):
            acc = acc + pbuf[:, POOL_HIST_PAD - k:POOL_HIST_PAD - k + rows, cols]
        cnt = jnp.minimum(pos + 1.0, float(w))
        diff = (acc / cnt - tok).reshape(tt, POOL_GROUP_DIM)
        pool_parts.append(_dot(diff.astype(_BF16), pool_w_ref[g]))
    pool_out = jnp.concatenate(pool_parts, axis=-1) * pool_scale_ref[...]
    hist_out_ref[...] = pbuf[:, rows:rows + POOL_HIST_PAD, :]
    if carry:
        pbuf[:, 0:POOL_HIST_PAD, :] = pbuf[:, rows:rows + POOL_HIST_PAD, :]

    cw = ws_ref.shape[1]
    ri = lax.broadcasted_iota(jnp.int32, (cw, cw), 0)
    ci = lax.broadcasted_iota(jnp.int32, (cw, cw), 1)
    causal = (ri // chunk == ci // chunk) & (ci % chunk <= ri % chunk)
    gm_parts = []
    vn_parts = []
    for hh in range(GMLP_HEADS):
        cols = slice(hh * GMLP_HEAD_DIM, (hh + 1) * GMLP_HEAD_DIM)
        vn = _rms(v[:, cols], gmlp_g_ref[:, cols])
        vn_parts.append(vn)
        wm = jnp.where(causal, ws_ref[hh], 0.0).astype(_BF16)
        vnb = vn.astype(_BF16)
        sp = [_dot(wm, vnb[c * cw:(c + 1) * cw]) + bs_ref[:, hh:hh + 1] for c in range(tt // cw)]
        s = sp[0] if len(sp) == 1 else jnp.concatenate(sp, axis=0)
        gm_parts.append(u[:, cols] * s)
    if vn_ref is not None:
        vn_ref[...] = jnp.concatenate(vn_parts, axis=-1)

    mix_in = jnp.concatenate([pool_out] + gm_parts, axis=-1)
    mix = _dot(mix_in.astype(_BF16), w_out_ref[...])
    x1 = x + gate1 * mix
    x1_ref[...] = x1

    h2 = _rms(x1, g2_ref[...]) * (1.0 + scale2) + shift2
    h2t = h2.T.astype(_BF16)
    h2t_ref[...] = h2t
    qt_scr[...] = _dot(wqt_ref[...], h2t)

    def score_body(hp, c):
        q = qt_scr[pl.ds(pl.multiple_of(hp * PEER_DHALF, PEER_DHALF), PEER_DHALF), :]
        s = _dot(keys_ref[hp], q.astype(_BF16))
        s_scr[hp] = s

        def keep(i, row):
            sv_scr[hp, i:i + 1, :] = row
        r_scr[hp] = _top_desc(s, PEER_TOPK, keep, with_rank=True)
        return c
    lax.fori_loop(0, 2 * PEER_HEADS, score_body, 0)

    row8 = lax.broadcasted_iota(jnp.int32, (SUBLANES, tt), 0)
    row16 = lax.broadcasted_iota(jnp.int32, (PEER_TOPK, tt), 0)

    def by_chunk(ref, hd, val):
        for cc in range(tt // LANES):
            ref[hd, cc] = val[:, cc * LANES:(cc + 1) * LANES]

    def head_body(hd, c):
        sv1 = sv_scr[2 * hd]
        sv2 = sv_scr[2 * hd + 1]
        cands = [sv1[0:1] + sv2]
        for a in range(1, PEER_TOPK):
            nb = PEER_TOPK // (a + 1)
            cands.append(jnp.where(row8 < nb, sv1[a:a + 1] + sv2[0:SUBLANES], -jnp.inf))
        cand = jnp.concatenate(cands, axis=0)
        last = []
        _top_desc(cand, PEER_TOPK, lambda i, row: last.append(row))
        thr = last[PEER_TOPK - 1]
        mx = sv1[0:1] + sv2[0:1]
        zsum = jnp.sum(jnp.where(cand >= thr, jnp.exp(cand - mx), 0.0), axis=0, keepdims=True)

        s1 = s_scr[2 * hd]
        s2 = s_scr[2 * hd + 1]
        cnt = jnp.zeros(s1.shape, _F32)
        for a in range(PEER_TOPK):
            nb = PEER_TOPK // (a + 1)
            part, rws = (sv2, row16) if nb > SUBLANES else (sv2[0:SUBLANES], row8)
            ok = ((sv1[a:a + 1] + part) >= thr) & (rws < nb)
            n_sel = jnp.sum(jnp.where(ok, 1.0, 0.0), axis=0, keepdims=True)
            cnt = jnp.where(s1 == sv1[a:a + 1], n_sel, cnt)
        by_chunk(cnt_ref, hd, cnt)
        by_chunk(r2_ref, hd, _pack_pairs(r_scr[2 * hd + 1]))
        by_chunk(e1_ref, hd, jnp.exp(s1 - sv1[0:1]))
        by_chunk(e2_ref, hd, _pack_pairs(jnp.exp(s2 - sv2[0:1]) / zsum))
        return c
    lax.fori_loop(0, PEER_HEADS, head_body, 0)


def _mixer(x, mod, hist, g1, w_in, pool_w, pool_scale, gmlp_g, ws, bs, w_out, g2, wqt, keys,
           *, batch, seq, seqs, rows, chunk, pos0, emit_vn):
    n, d = x.shape
    tt = seqs * rows
    steps = seq // rows
    carry = steps > 1
    nb = batch // seqs
    pool_width = pool_w.shape[0] * POOL_GROUP_DIM
    gmlp_width = GMLP_HEADS * GMLP_HEAD_DIM
    hp2 = 2 * PEER_HEADS
    mod_rows = mod.shape[1]
    tok = lambda b, t: (b * steps + t, 0)
    const2 = lambda b, t: (0, 0)
    const3 = lambda b, t: (0, 0, 0)
    if mod_rows == 1:
        mod_spec = pl.BlockSpec((N_MOD, 1, d), lambda b, t: (b, 0, 0))
    else:
        mod_spec = pl.BlockSpec((N_MOD, tt, d), lambda b, t: (0, b * steps + t, 0))
    in_specs = [
        pl.BlockSpec((tt, d), tok),
        mod_spec,
        pl.BlockSpec((seqs, POOL_HIST_PAD, pool_width), lambda b, t: (b, 0, 0)),
        pl.BlockSpec((1, d), const2),
        pl.BlockSpec(w_in.shape, const2),
        pl.BlockSpec(pool_w.shape, const3),
        pl.BlockSpec((1, pool_width), const2),
        pl.BlockSpec((1, gmlp_width), const2),
        pl.BlockSpec(ws.shape, const3),
        pl.BlockSpec(bs.shape, const2),
        pl.BlockSpec(w_out.shape, const2),
        pl.BlockSpec((1, d), const2),
        pl.BlockSpec(wqt.shape, const2),
        pl.BlockSpec(keys.shape, const3),
    ]
    def by_chunk(rows_, dt):
        spec = pl.BlockSpec((PEER_HEADS, tt // LANES, rows_, LANES),
                            lambda b, t: (0, b * steps + t, 0, 0))
        return spec, jax.ShapeDtypeStruct((PEER_HEADS, n // LANES, rows_, LANES), dt)
    (pair_spec, pair_shape), (key_spec, key_shape) = (
        by_chunk(PEER_NKEYS // 2, jnp.uint32), by_chunk(PEER_NKEYS, _F32))
    out_specs = [
        pl.BlockSpec((tt, d), tok),
        pl.BlockSpec((d, tt), lambda b, t: (0, b * steps + t)),
        pair_spec, key_spec, key_spec, pair_spec,
        pl.BlockSpec((seqs, POOL_HIST_PAD, pool_width), lambda b, t: (b, 0, 0)),
    ]
    out_shape = [
        jax.ShapeDtypeStruct((n, d), _F32),
        jax.ShapeDtypeStruct((d, n), _BF16),
        pair_shape, key_shape, key_shape, pair_shape,
        jax.ShapeDtypeStruct((batch, POOL_HIST_PAD, pool_width), _F32),
    ]
    if emit_vn:
        out_specs.append(pl.BlockSpec((tt, gmlp_width), tok))
        out_shape.append(jax.ShapeDtypeStruct((n, gmlp_width), _F32))
    body = functools.partial(_mixer_kernel, seqs=seqs, rows=rows, chunk=chunk, pos0=pos0,
                             carry=carry, emit_vn=emit_vn)
    return pl.pallas_call(
        body,
        grid=(nb, steps),
        in_specs=in_specs,
        out_specs=out_specs,
        out_shape=out_shape,
        scratch_shapes=[
            pltpu.VMEM((seqs, POOL_HIST_PAD + rows, pool_width), _F32),
            pltpu.VMEM((hp2 * PEER_DHALF, tt), _F32),
            pltpu.VMEM((hp2, PEER_NKEYS, tt), _F32),
            pltpu.VMEM((hp2, PEER_NKEYS, tt), _F32),
            pltpu.VMEM((hp2, PEER_TOPK, tt), _F32),
        ],
        compiler_params=pltpu.CompilerParams(
            dimension_semantics=("arbitrary", "arbitrary"), vmem_limit_bytes=VMEM_LIMIT_BYTES),
        name="mixer_seq" if carry else "mixer_step",
    )(x, mod, hist, g1, w_in, pool_w, pool_scale, gmlp_g, ws, bs, w_out, g2, wqt, keys)


def _peer_kernel(h2t_ref, r2_ref, cnt_ref, e1_ref, e2_ref, u_ref, vt_ref, x1_ref,
                 gate_ref, gf_ref, y_ref, a_scr, w_scr, acc_scr, *, n_k, final_norm):
    s = pl.program_id(0)
    _, te, tn = a_scr.shape
    d = acc_scr.shape[0]
    n_i1 = te // PEER_NKEYS
    k_red = jnp.maximum(s - 2, 0) % n_k

    @pl.when(s == 0)
    def _():
        a_scr[...] = jnp.zeros_like(a_scr)
        w_scr[...] = jnp.zeros_like(w_scr)

    def step(cur):
        prev = 1 - cur

        def score_rows(m):
            rws = slice(m * MXU_ROWS, (m + 1) * MXU_ROWS)
            a_scr[cur, rws, :] = _dot(u_ref[rws, :], h2t_ref[...])

        def reduce_rows(m):
            rws = slice(m * MXU_ROWS, (m + 1) * MXU_ROWS)
            contrib = _dot(vt_ref[rws, :], w_scr[cur])
            acc_scr[rws, :] = contrib + jnp.where(k_red > 0, acc_scr[rws, :], 0.0)

        mxu_work = ([functools.partial(score_rows, m) for m in range(te // MXU_ROWS)]
                    + [functools.partial(reduce_rows, m) for m in range(d // MXU_ROWS)])

        def all_rows(row):
            tile = jnp.broadcast_to(row, (BF16_ROWS, LANES)).astype(_BF16)
            return jnp.broadcast_to(tile[None], (PEER_NKEYS // BF16_ROWS, BF16_ROWS, LANES)
                                    ).reshape(PEER_NKEYS, LANES)

        def gate_block(cc, j):
            lanes = slice(cc * LANES, (cc + 1) * LANES)
            rws = slice(j * PEER_NKEYS, (j + 1) * PEER_NKEYS)
            gate = None
            for hd in range(PEER_HEADS):
                sel = _unpack_pairs(r2_ref[hd, cc]) < all_rows(cnt_ref[hd, cc, j:j + 1, :])
                term = jnp.where(
                    sel,
                    _unpack_pairs(e2_ref[hd, cc]) * all_rows(e1_ref[hd, cc, j:j + 1, :]),
                    jnp.zeros((), _BF16))
                gate = term if gate is None else gate + term
            act = _gelu(a_scr[prev, rws, lanes])
            w_scr[prev, rws, lanes] = gate * act.astype(_BF16)

        vpu_work = [functools.partial(gate_block, cc, j)
                    for cc in range(tn // LANES) for j in range(n_i1)]

        per_piece = -(-len(vpu_work) // len(mxu_work))
        for i, piece in enumerate(mxu_work):
            piece()
            for blk in vpu_work[i * per_piece:(i + 1) * per_piece]:
                blk()

    for parity in range(2):
        pl.when(s % 2 == parity)(functools.partial(step, parity))

    @pl.when((s >= 2) & (k_red == n_k - 1))
    def _():
        x = x1_ref[...] + gate_ref[...] * acc_scr[...].T
        y_ref[...] = _rms(x, gf_ref[...]) if final_norm else x


def _peer(h2t, r2, cnt, e1, e2, u_tab, vt_tab, x1, gate2, g_final, *, tn, tokens_per_gate,
          final_norm):
    d, n = h2t.shape
    n_exp = u_tab.shape[0]
    te = PEER_EXPERTS_TILE
    n_i1 = te // PEER_NKEYS
    n_k = n_exp // te
    pairs = (n // tn) * n_k

    def stage(lag):
        def split(s):
            p = jnp.clip(s - lag, 0, pairs - 1)
            return p // n_k, p % n_k
        return split
    score, gating, reduce_ = stage(0), stage(1), stage(2)

    gate_rows = gate2.shape[0]
    if gate_rows == n:
        gate_spec = pl.BlockSpec((tn, d), lambda s: (reduce_(s)[0], 0))
    else:
        gate_spec = pl.BlockSpec(
            (None, 1, d), lambda s: ((reduce_(s)[0] * tn) // tokens_per_gate, 0, 0))
        gate2 = gate2.reshape(gate_rows, 1, d)
    per_i1 = pl.BlockSpec((PEER_HEADS, tn // LANES, n_i1, LANES),
                          lambda s: (0, gating(s)[0], gating(s)[1], 0))
    per_i2 = pl.BlockSpec((PEER_HEADS, tn // LANES, PEER_NKEYS // 2, LANES),
                          lambda s: (0, gating(s)[0], 0, 0))
    return pl.pallas_call(
        functools.partial(_peer_kernel, n_k=n_k, final_norm=final_norm),
        grid=(pairs + 2,),
        in_specs=[
            pl.BlockSpec((d, tn), lambda s: (0, score(s)[0])),
            per_i2, per_i1, per_i1, per_i2,
            pl.BlockSpec((te, d), lambda s: (score(s)[1], 0)),
            pl.BlockSpec((d, te), lambda s: (0, reduce_(s)[1])),
            pl.BlockSpec((tn, d), lambda s: (reduce_(s)[0], 0)),
            gate_spec,
            pl.BlockSpec((1, d), lambda s: (0, 0)),
        ],
        out_specs=pl.BlockSpec((tn, d), lambda s: (reduce_(s)[0], 0)),
        out_shape=jax.ShapeDtypeStruct((n, d), _F32),
        scratch_shapes=[
            pltpu.VMEM((2, te, tn), _F32),
            pltpu.VMEM((2, te, tn), _BF16),
            pltpu.VMEM((d, tn), _F32),
        ],
        compiler_params=pltpu.CompilerParams(
            dimension_semantics=("arbitrary",), vmem_limit_bytes=VMEM_LIMIT_BYTES),
        name="peer_prompt" if tn == PEER_TOKENS else "peer_step",
    )(h2t, r2, cnt, e1, e2, u_tab, vt_tab, x1, gate2, g_final.reshape(1, d))


def kernel(x_prompt, x_sample, cache_pool, c_prompt, c_sample, w_ada, b_ada, g_norm1, w_in,
           pool_w, pool_scale, gmlp_norm_g, gmlp_ws, gmlp_bs, w_out, g_norm2, peer_wq,
           peer_keys, peer_u, peer_v, g_final):
    depth = w_ada.shape[0]
    batch, seq, d = x_prompt.shape
    dec_batch, dec_seq, _ = x_sample.shape
    past_len = 2048
    n_p, n_s = batch * seq, dec_batch * dec_seq
    pool_width = cache_pool.shape[-1]
    assert seq % MIXER_TOKENS == 0 and MIXER_TOKENS % GMLP_CHUNK == 0
    assert dec_seq % SUBLANES == 0 and dec_seq <= GMLP_CHUNK and POOL_HIST_PAD <= dec_seq
    assert n_p % PEER_TOKENS == 0 and seq % PEER_TOKENS == 0

    xp = x_prompt.reshape(n_p, d)
    xs = x_sample.reshape(n_s, d)
    c_rows = batch + dec_batch
    c_pad = -(-c_rows // SUBLANES) * SUBLANES
    c_all = jnp.concatenate([c_prompt, c_sample, jnp.zeros((c_pad - c_rows, d), _F32)], axis=0)
    reps = n_s // dec_seq

    pool_p, pool_s, vn_s = [], [], []
    for l in range(depth):
        mod = _ada(c_all, w_ada[l], b_ada[l]).reshape(c_pad, N_MOD, d)
        mod_p = mod[:batch].reshape(batch * N_MOD, 1, d)
        mod_s = jnp.repeat(mod[batch:c_rows].transpose(1, 0, 2), dec_seq, axis=1)
        gate2_p = mod[:batch, 5]
        gate2_s = mod_s[5]

        w_in_b = w_in[l].astype(_BF16)
        pool_w_b = pool_w[l].astype(_BF16)
        w_out_b = w_out[l].astype(_BF16)
        wqt_b = peer_wq[l].T.astype(_BF16)
        keys_b = peer_keys[l].reshape(2 * PEER_HEADS, PEER_NKEYS, PEER_DHALF).astype(_BF16)
        u_b = peer_u[l].astype(_BF16)
        vt_b = peer_v[l].T.astype(_BF16)
        g1 = g_norm1[l].reshape(1, d)
        g2 = g_norm2[l].reshape(1, d)
        pscale = pool_scale[l].reshape(1, pool_width)
        gg = gmlp_norm_g[l].reshape(1, -1)
        final = l == depth - 1

        common = (g1, w_in_b, pool_w_b, pscale, gg)
        tail = (w_out_b, g2, wqt_b, keys_b)

        hist0 = jnp.zeros((batch, POOL_HIST_PAD, pool_width), _F32)
        x1, h2t, r2, cnt, e1, e2, hist_p = _mixer(
            xp, mod_p, hist0, *common, gmlp_ws[l], gmlp_bs[l].T, *tail,
            batch=batch, seq=seq, seqs=1, rows=MIXER_TOKENS, chunk=GMLP_CHUNK, pos0=0,
            emit_vn=False)
        xp = _peer(h2t, r2, cnt, e1, e2, u_b, vt_b, x1, gate2_p, g_final, tn=PEER_TOKENS,
                   tokens_per_gate=seq, final_norm=final)

        hist_s = jnp.pad(cache_pool[l], ((0, 0), (POOL_HIST_PAD - cache_pool.shape[2], 0), (0, 0)))
        ws_s = jnp.tile(gmlp_ws[l][:, :dec_seq, :dec_seq], (1, reps, reps))
        bs_s = jnp.tile(gmlp_bs[l][:, :dec_seq], (1, reps)).T
        x1, h2t, r2, cnt, e1, e2, hist_sn, vn = _mixer(
            xs, mod_s, hist_s, *common, ws_s, bs_s, *tail,
            batch=dec_batch, seq=dec_seq, seqs=dec_batch, rows=dec_seq, chunk=dec_seq,
            pos0=past_len, emit_vn=True)
        xs = _peer(h2t, r2, cnt, e1, e2, u_b, vt_b, x1, gate2_s, g_final, tn=n_s,
                   tokens_per_gate=1, final_norm=final)

        pool_p.append(hist_p[:, 1:])
        pool_s.append(hist_sn[:, 1:])
        vn_s.append(vn.reshape(dec_batch, dec_seq, -1))

    return (xp.reshape(batch, seq, d), xs.reshape(dec_batch, dec_seq, d),
            jnp.stack(pool_p), jnp.stack(pool_s), jnp.stack(vn_s))
```

```python
import functools

import jax
import jax.numpy as jnp
from jax import lax
from jax.experimental import pallas as pl
from jax.experimental.pallas import tpu as pltpu

EPS = 1e-6
POOL_WINDOWS = (2, 4, 8, 16)
POOL_GROUP_DIM = 128
POOL_HIST_PAD = 16
GMLP_HEADS = 4
GMLP_HEAD_DIM = 128
GMLP_CHUNK = 128
PEER_HEADS = 8
PEER_NKEYS = 128
PEER_TOPK = 16
PEER_DHALF = 128
UNRANKED = 127.0
N_MOD = 6

LANES = 128
SUBLANES = 8
BF16_ROWS = 16
VMEM_LIMIT_BYTES = 56 * 1024 * 1024

MIXER_TOKENS = 256
PEER_TOKENS = 512
PEER_EXPERTS_TILE = 2048
MXU_ROWS = 128

_BF16 = jnp.bfloat16
_F32 = jnp.float32


def _gelu(x):
    return 0.5 * x * (1.0 + lax.erf(x * 0.7071067811865476))


def _rms(x, g):
    return x * lax.rsqrt(jnp.mean(x * x, axis=-1, keepdims=True) + EPS) * g


def _dot(a, b):
    return jnp.dot(a, b, preferred_element_type=_F32)


def _pack_pairs(x):
    return pltpu.bitcast(x.astype(_BF16), jnp.uint32)


def _unpack_pairs(x):
    return pltpu.bitcast(x, _BF16)


def _ada_kernel(c_ref, w_ref, b_ref, o_ref):
    c = c_ref[...]
    sc = c * jax.nn.sigmoid(c)
    o_ref[...] = _dot(sc.astype(_BF16), w_ref[...].astype(_BF16)) + b_ref[...]


def _ada(c, w, b):
    rows, d = c.shape
    n = w.shape[1]
    tile = 1536
    return pl.pallas_call(
        _ada_kernel,
        grid=(n // tile,),
        in_specs=[pl.BlockSpec((rows, d), lambda j: (0, 0)),
                  pl.BlockSpec((d, tile), lambda j: (0, j)),
                  pl.BlockSpec((1, tile), lambda j: (0, j))],
        out_specs=pl.BlockSpec((rows, tile), lambda j: (0, j)),
        out_shape=jax.ShapeDtypeStruct((rows, n), _F32),
        compiler_params=pltpu.CompilerParams(
            dimension_semantics=("arbitrary",), vmem_limit_bytes=VMEM_LIMIT_BYTES),
        name="adaln",
    )(c, w, b.reshape(1, n))


def _top_desc(s, k, emit, with_rank=False):
    cur = s
    rank = jnp.full(s.shape, UNRANKED, _F32) if with_rank else None
    for i in range(k):
        m = jnp.max(cur, axis=0, keepdims=True)
        emit(i, m)
        if i + 1 < k or with_rank:
            hit = cur == m
            if with_rank:
                rank = jnp.where(hit, float(i), rank)
            cur = jnp.where(hit, -jnp.inf, cur)
    return rank


def _mixer_kernel(x_ref, mod_ref, hist_ref, g1_ref, w_in_ref, pool_w_ref, pool_scale_ref,
                  gmlp_g_ref, ws_ref, bs_ref, w_out_ref, g2_ref, wqt_ref, keys_ref,
                  *rest, seqs, rows, chunk, pos0, carry, emit_vn):
    if emit_vn:
        (x1_ref, h2t_ref, r2_ref, cnt_ref, e1_ref, e2_ref, hist_out_ref, vn_ref,
         pbuf, qt_scr, s_scr, r_scr, sv_scr) = rest
    else:
        (x1_ref, h2t_ref, r2_ref, cnt_ref, e1_ref, e2_ref, hist_out_ref,
         pbuf, qt_scr, s_scr, r_scr, sv_scr) = rest
        vn_ref = None
    t = pl.program_id(1)
    tt = seqs * rows
    pool_width = pool_w_ref.shape[0] * POOL_GROUP_DIM
    gmlp_width = GMLP_HEADS * GMLP_HEAD_DIM

    x = x_ref[...]
    shift1, scale1, gate1, shift2, scale2 = (mod_ref[i] for i in range(5))
    h = _rms(x, g1_ref[...]) * (1.0 + scale1) + shift1
    z = _dot(h.astype(_BF16), w_in_ref[...])
    p = z[:, :pool_width]
    u = _gelu(z[:, pool_width:pool_width + gmlp_width])
    v = _gelu(z[:, pool_width + gmlp_width:])

    if carry:
        @pl.when(t == 0)
        def _():
            pbuf[:, 0:POOL_HIST_PAD, :] = hist_ref[...]
    else:
        pbuf[:, 0:POOL_HIST_PAD, :] = hist_ref[...]
    p3 = p.reshape(seqs, rows, pool_width)
    pbuf[:, POOL_HIST_PAD:POOL_HIST_PAD + rows, :] = p3
    pos = (pos0 + t * rows + lax.broadcasted_iota(jnp.int32, (1, rows, 1), 1)).astype(_F32)
    pool_parts = []
    for g, w in enumerate(POOL_WINDOWS):
        cols = slice(g * POOL_GROUP_DIM, (g + 1) * POOL_GROUP_DIM)
        tok = p3[:, :, cols]
        acc = tok
        for k in range(1, w):
            acc = acc + pbuf[:, POOL_HIST_PAD - k:POOL_HIST_PAD - k + rows, cols]
        cnt = jnp.minimum(pos + 1.0, float(w))
        diff = (acc / cnt - tok).reshape(tt, POOL_GROUP_DIM)
        pool_parts.append(_dot(diff.astype(_BF16), pool_w_ref[g]))
    pool_out = jnp.concatenate(pool_parts, axis=-1) * pool_scale_ref[...]
    hist_out_ref[...] = pbuf[:, rows:rows + POOL_HIST_PAD, :]
    if carry:
        pbuf[:, 0:POOL_HIST_PAD, :] = pbuf[:, rows:rows + POOL_HIST_PAD, :]

    cw = ws_ref.shape[1]
    ri = lax.broadcasted_iota(jnp.int32, (cw, cw), 0)
    ci = lax.broadcasted_iota(jnp.int32, (cw, cw), 1)
    causal = (ri // chunk == ci // chunk) & (ci % chunk <= ri % chunk)
    gm_parts = []
    vn_parts = []
    for hh in range(GMLP_HEADS):
        cols = slice(hh * GMLP_HEAD_DIM, (hh + 1) * GMLP_HEAD_DIM)
        vn = _rms(v[:, cols], gmlp_g_ref[:, cols])
        vn_parts.append(vn)
        wm = jnp.where(causal, ws_ref[hh], 0.0).astype(_BF16)
        vnb = vn.astype(_BF16)
        sp = [_dot(wm, vnb[c * cw:(c + 1) * cw]) + bs_ref[:, hh:hh + 1] for c in range(tt // cw)]
        s = sp[0] if len(sp) == 1 else jnp.concatenate(sp, axis=0)
        gm_parts.append(u[:, cols] * s)
    if vn_ref is not None:
        vn_ref[...] = jnp.concatenate(vn_parts, axis=-1)

    mix_in = jnp.concatenate([pool_out] + gm_parts, axis=-1)
    mix = _dot(mix_in.astype(_BF16), w_out_ref[...])
    x1 = x + gate1 * mix
    x1_ref[...] = x1

    h2 = _rms(x1, g2_ref[...]) * (1.0 + scale2) + shift2
    h2t = h2.T.astype(_BF16)
    h2t_ref[...] = h2t
    qt_scr[...] = _dot(wqt_ref[...], h2t)

    def score_body(hp, c):
        q = qt_scr[pl.ds(pl.multiple_of(hp * PEER_DHALF, PEER_DHALF), PEER_DHALF), :]
        s = _dot(keys_ref[hp], q.astype(_BF16))
        s_scr[hp] = s

        def keep(i, row):
            sv_scr[hp, i:i + 1, :] = row
        r_scr[hp] = _top_desc(s, PEER_TOPK, keep, with_rank=True)
        return c
    lax.fori_loop(0, 2 * PEER_HEADS, score_body, 0, unroll=2)

    row8 = lax.broadcasted_iota(jnp.int32, (SUBLANES, tt), 0)
    row16 = lax.broadcasted_iota(jnp.int32, (PEER_TOPK, tt), 0)

    def by_chunk(ref, hd, val):
        for cc in range(tt // LANES):
            ref[hd, cc] = val[:, cc * LANES:(cc + 1) * LANES]

    def head_body(hd, c):
        sv1 = sv_scr[2 * hd]
        sv2 = sv_scr[2 * hd + 1]
        cands = [sv1[0:1] + sv2]
        for a in range(1, PEER_TOPK):
            nb = PEER_TOPK // (a + 1)
            cands.append(jnp.where(row8 < nb, sv1[a:a + 1] + sv2[0:SUBLANES], -jnp.inf))
        cand = jnp.concatenate(cands, axis=0)
        last = []
        _top_desc(cand, PEER_TOPK, lambda i, row: last.append(row))
        thr = last[PEER_TOPK - 1]
        mx = sv1[0:1] + sv2[0:1]
        zsum = jnp.sum(jnp.where(cand >= thr, jnp.exp(cand - mx), 0.0), axis=0, keepdims=True)

        s1 = s_scr[2 * hd]
        s2 = s_scr[2 * hd + 1]
        cnt = jnp.zeros(s1.shape, _F32)
        for a in range(PEER_TOPK):
            nb = PEER_TOPK // (a + 1)
            part, rws = (sv2, row16) if nb > SUBLANES else (sv2[0:SUBLANES], row8)
            ok = ((sv1[a:a + 1] + part) >= thr) & (rws < nb)
            n_sel = jnp.sum(jnp.where(ok, 1.0, 0.0), axis=0, keepdims=True)
            cnt = jnp.where(s1 == sv1[a:a + 1], n_sel, cnt)
        by_chunk(cnt_ref, hd, cnt)
        by_chunk(r2_ref, hd, _pack_pairs(r_scr[2 * hd + 1]))
        by_chunk(e1_ref, hd, jnp.exp(s1 - sv1[0:1]))
        by_chunk(e2_ref, hd, _pack_pairs(jnp.exp(s2 - sv2[0:1]) / zsum))
        return c
    lax.fori_loop(0, PEER_HEADS, head_body, 0, unroll=2)


def _mixer(x, mod, hist, g1, w_in, pool_w, pool_scale, gmlp_g, ws, bs, w_out, g2, wqt, keys,
           *, batch, seq, seqs, rows, chunk, pos0, emit_vn):
    n, d = x.shape
    tt = seqs * rows
    steps = seq // rows
    carry = steps > 1
    nb = batch // seqs
    pool_width = pool_w.shape[0] * POOL_GROUP_DIM
    gmlp_width = GMLP_HEADS * GMLP_HEAD_DIM
    hp2 = 2 * PEER_HEADS
    mod_rows = mod.shape[1]
    tok = lambda b, t: (b * steps + t, 0)
    const2 = lambda b, t: (0, 0)
    const3 = lambda b, t: (0, 0, 0)
    if mod_rows == 1:
        mod_spec = pl.BlockSpec((N_MOD, 1, d), lambda b, t: (b, 0, 0))
    else:
        mod_spec = pl.BlockSpec((N_MOD, tt, d), lambda b, t: (0, b * steps + t, 0))
    in_specs = [
        pl.BlockSpec((tt, d), tok),
        mod_spec,
        pl.BlockSpec((seqs, POOL_HIST_PAD, pool_width), lambda b, t: (b, 0, 0)),
        pl.BlockSpec((1, d), const2),
        pl.BlockSpec(w_in.shape, const2),
        pl.BlockSpec(pool_w.shape, const3),
        pl.BlockSpec((1, pool_width), const2),
        pl.BlockSpec((1, gmlp_width), const2),
        pl.BlockSpec(ws.shape, const3),
        pl.BlockSpec(bs.shape, const2),
        pl.BlockSpec(w_out.shape, const2),
        pl.BlockSpec((1, d), const2),
        pl.BlockSpec(wqt.shape, const2),
        pl.BlockSpec(keys.shape, const3),
    ]
    def by_chunk(rows_, dt):
        spec = pl.BlockSpec((PEER_HEADS, tt // LANES, rows_, LANES),
                            lambda b, t: (0, b * steps + t, 0, 0))
        return spec, jax.ShapeDtypeStruct((PEER_HEADS, n // LANES, rows_, LANES), dt)
    (pair_spec, pair_shape), (key_spec, key_shape) = (
        by_chunk(PEER_NKEYS // 2, jnp.uint32), by_chunk(PEER_NKEYS, _F32))
    out_specs = [
        pl.BlockSpec((tt, d), tok),
        pl.BlockSpec((d, tt), lambda b, t: (0, b * steps + t)),
        pair_spec, key_spec, key_spec, pair_spec,
        pl.BlockSpec((seqs, POOL_HIST_PAD, pool_width), lambda b, t: (b, 0, 0)),
    ]
    out_shape = [
        jax.ShapeDtypeStruct((n, d), _F32),
        jax.ShapeDtypeStruct((d, n), _BF16),
        pair_shape, key_shape, key_shape, pair_shape,
        jax.ShapeDtypeStruct((batch, POOL_HIST_PAD, pool_width), _F32),
    ]
    if emit_vn:
        out_specs.append(pl.BlockSpec((tt, gmlp_width), tok))
        out_shape.append(jax.ShapeDtypeStruct((n, gmlp_width), _F32))
    body = functools.partial(_mixer_kernel, seqs=seqs, rows=rows, chunk=chunk, pos0=pos0,
                             carry=carry, emit_vn=emit_vn)
    return pl.pallas_call(
        body,
        grid=(nb, steps),
        in_specs=in_specs,
        out_specs=out_specs,
        out_shape=out_shape,
        scratch_shapes=[
            pltpu.VMEM((seqs, POOL_HIST_PAD + rows, pool_width), _F32),
            pltpu.VMEM((hp2 * PEER_DHALF, tt), _F32),
            pltpu.VMEM((hp2, PEER_NKEYS, tt), _F32),
            pltpu.VMEM((hp2, PEER_NKEYS, tt), _F32),
            pltpu.VMEM((hp2, PEER_TOPK, tt), _F32),
        ],
        compiler_params=pltpu.CompilerParams(
            dimension_semantics=("arbitrary", "arbitrary"), vmem_limit_bytes=VMEM_LIMIT_BYTES),
        name="mixer_seq" if carry else "mixer_step",
    )(x, mod, hist, g1, w_in, pool_w, pool_scale, gmlp_g, ws, bs, w_out, g2, wqt, keys)


def _peer_kernel(h2t_ref, r2_ref, cnt_ref, e1_ref, e2_ref, u_ref, vt_ref, x1_ref,
                 gate_ref, gf_ref, y_ref, a_scr, w_scr, acc_scr, *, n_k, final_norm):
    s = pl.program_id(0)
    _, te, tn = a_scr.shape
    d = acc_scr.shape[0]
    n_i1 = te // PEER_NKEYS
    k_red = jnp.maximum(s - 2, 0) % n_k

    @pl.when(s == 0)
    def _():
        a_scr[...] = jnp.zeros_like(a_scr)
        w_scr[...] = jnp.zeros_like(w_scr)

    def step(cur):
        prev = 1 - cur

        def score_rows(m):
            rws = slice(m * MXU_ROWS, (m + 1) * MXU_ROWS)
            a_scr[cur, rws, :] = _dot(u_ref[rws, :], h2t_ref[...])

        def reduce_rows(m):
            rws = slice(m * MXU_ROWS, (m + 1) * MXU_ROWS)
            contrib = _dot(vt_ref[rws, :], w_scr[cur])
            acc_scr[rws, :] = contrib + jnp.where(k_red > 0, acc_scr[rws, :], 0.0)

        mxu_work = ([functools.partial(score_rows, m) for m in range(te // MXU_ROWS)]
                    + [functools.partial(reduce_rows, m) for m in range(d // MXU_ROWS)])

        def all_rows(row):
            tile = jnp.broadcast_to(row, (BF16_ROWS, LANES)).astype(_BF16)
            return jnp.broadcast_to(tile[None], (PEER_NKEYS // BF16_ROWS, BF16_ROWS, LANES)
                                    ).reshape(PEER_NKEYS, LANES)

        def gate_block(cc, j):
            lanes = slice(cc * LANES, (cc + 1) * LANES)
            rws = slice(j * PEER_NKEYS, (j + 1) * PEER_NKEYS)
            gate = None
            for hd in range(PEER_HEADS):
                sel = _unpack_pairs(r2_ref[hd, cc]) < all_rows(cnt_ref[hd, cc, j:j + 1, :])
                term = jnp.where(
                    sel,
                    _unpack_pairs(e2_ref[hd, cc]) * all_rows(e1_ref[hd, cc, j:j + 1, :]),
                    jnp.zeros((), _BF16))
                gate = term if gate is None else gate + term
            act = _gelu(a_scr[prev, rws, lanes])
            w_scr[prev, rws, lanes] = gate * act.astype(_BF16)

        vpu_work = [functools.partial(gate_block, cc, j)
                    for cc in range(tn // LANES) for j in range(n_i1)]

        per_piece = -(-len(vpu_work) // len(mxu_work))
        for i, piece in enumerate(mxu_work):
            piece()
            for blk in vpu_work[i * per_piece:(i + 1) * per_piece]:
                blk()

    for parity in range(2):
        pl.when(s % 2 == parity)(functools.partial(step, parity))

    @pl.when((s >= 2) & (k_red == n_k - 1))
    def _():
        x = x1_ref[...] + gate_ref[...] * acc_scr[...].T
        y_ref[...] = _rms(x, gf_ref[...]) if final_norm else x


def _peer(h2t, r2, cnt, e1, e2, u_tab, vt_tab, x1, gate2, g_final, *, tn, tokens_per_gate,
          final_norm):
    d, n = h2t.shape
    n_exp = u_tab.shape[0]
    te = PEER_EXPERTS_TILE
    n_i1 = te // PEER_NKEYS
    n_k = n_exp // te
    pairs = (n // tn) * n_k

    def stage(lag):
        def split(s):
            p = jnp.clip(s - lag, 0, pairs - 1)
            return p // n_k, p % n_k
        return split
    score, gating, reduce_ = stage(0), stage(1), stage(2)

    gate_rows = gate2.shape[0]
    if gate_rows == n:
        gate_spec = pl.BlockSpec((tn, d), lambda s: (reduce_(s)[0], 0))
    else:
        gate_spec = pl.BlockSpec(
            (None, 1, d), lambda s: ((reduce_(s)[0] * tn) // tokens_per_gate, 0, 0))
        gate2 = gate2.reshape(gate_rows, 1, d)
    per_i1 = pl.BlockSpec((PEER_HEADS, tn // LANES, n_i1, LANES),
                          lambda s: (0, gating(s)[0], gating(s)[1], 0))
    per_i2 = pl.BlockSpec((PEER_HEADS, tn // LANES, PEER_NKEYS // 2, LANES),
                          lambda s: (0, gating(s)[0], 0, 0))
    return pl.pallas_call(
        functools.partial(_peer_kernel, n_k=n_k, final_norm=final_norm),
        grid=(pairs + 2,),
        in_specs=[
            pl.BlockSpec((d, tn), lambda s: (0, score(s)[0])),
            per_i2, per_i1, per_i1, per_i2,
            pl.BlockSpec((te, d), lambda s: (score(s)[1], 0)),
            pl.BlockSpec((d, te), lambda s: (0, reduce_(s)[1])),
            pl.BlockSpec((tn, d), lambda s: (reduce_(s)[0], 0)),
            gate_spec,
            pl.BlockSpec((1, d), lambda s: (0, 0)),
        ],
        out_specs=pl.BlockSpec((tn, d), lambda s: (reduce_(s)[0], 0)),
        out_shape=jax.ShapeDtypeStruct((n, d), _F32),
        scratch_shapes=[
            pltpu.VMEM((2, te, tn), _F32),
            pltpu.VMEM((2, te, tn), _BF16),
            pltpu.VMEM((d, tn), _F32),
        ],
        compiler_params=pltpu.CompilerParams(
            dimension_semantics=("arbitrary",), vmem_limit_bytes=VMEM_LIMIT_BYTES),
        name="peer_prompt" if tn == PEER_TOKENS else "peer_step",
    )(h2t, r2, cnt, e1, e2, u_tab, vt_tab, x1, gate2, g_final.reshape(1, d))


def kernel(x_prompt, x_sample, cache_pool, c_prompt, c_sample, w_ada, b_ada, g_norm1, w_in,
           pool_w, pool_scale, gmlp_norm_g, gmlp_ws, gmlp_bs, w_out, g_norm2, peer_wq,
           peer_keys, peer_u, peer_v, g_final):
    depth = w_ada.shape[0]
    batch, seq, d = x_prompt.shape
    dec_batch, dec_seq, _ = x_sample.shape
    past_len = 2048
    n_p, n_s = batch * seq, dec_batch * dec_seq
    pool_width = cache_pool.shape[-1]
    assert seq % MIXER_TOKENS == 0 and MIXER_TOKENS % GMLP_CHUNK == 0
    assert dec_seq % SUBLANES == 0 and dec_seq <= GMLP_CHUNK and POOL_HIST_PAD <= dec_seq
    assert n_p % PEER_TOKENS == 0 and seq % PEER_TOKENS == 0

    xp = x_prompt.reshape(n_p, d)
    xs = x_sample.reshape(n_s, d)
    c_rows = batch + dec_batch
    c_pad = -(-c_rows // SUBLANES) * SUBLANES
    c_all = jnp.concatenate([c_prompt, c_sample, jnp.zeros((c_pad - c_rows, d), _F32)], axis=0)
    reps = n_s // dec_seq

    pool_p, pool_s, vn_s = [], [], []
    for l in range(depth):
        mod = _ada(c_all, w_ada[l], b_ada[l]).reshape(c_pad, N_MOD, d)
        mod_p = mod[:batch].reshape(batch * N_MOD, 1, d)
        mod_s = jnp.repeat(mod[batch:c_rows].transpose(1, 0, 2), dec_seq, axis=1)
        gate2_p = mod[:batch, 5]
        gate2_s = mod_s[5]

        w_in_b = w_in[l].astype(_BF16)
        pool_w_b = pool_w[l].astype(_BF16)
        w_out_b = w_out[l].astype(_BF16)
        wqt_b = peer_wq[l].T.astype(_BF16)
        keys_b = peer_keys[l].reshape(2 * PEER_HEADS, PEER_NKEYS, PEER_DHALF).astype(_BF16)
        u_b = peer_u[l].astype(_BF16)
        vt_b = peer_v[l].T.astype(_BF16)
        g1 = g_norm1[l].reshape(1, d)
        g2 = g_norm2[l].reshape(1, d)
        pscale = pool_scale[l].reshape(1, pool_width)
        gg = gmlp_norm_g[l].reshape(1, -1)
        final = l == depth - 1

        common = (g1, w_in_b, pool_w_b, pscale, gg)
        tail = (w_out_b, g2, wqt_b, keys_b)

        hist0 = jnp.zeros((batch, POOL_HIST_PAD, pool_width), _F32)
        x1, h2t, r2, cnt, e1, e2, hist_p = _mixer(
            xp, mod_p, hist0, *common, gmlp_ws[l], gmlp_bs[l].T, *tail,
            batch=batch, seq=seq, seqs=1, rows=MIXER_TOKENS, chunk=GMLP_CHUNK, pos0=0,
            emit_vn=False)
        xp = _peer(h2t, r2, cnt, e1, e2, u_b, vt_b, x1, gate2_p, g_final, tn=PEER_TOKENS,
                   tokens_per_gate=seq, final_norm=final)

        hist_s = jnp.pad(cache_pool[l], ((0, 0), (POOL_HIST_PAD - cache_pool.shape[2], 0), (0, 0)))
        ws_s = jnp.tile(gmlp_ws[l][:, :dec_seq, :dec_seq], (1, reps, reps))
        bs_s = jnp.tile(gmlp_bs[l][:, :dec_seq], (1, reps)).T
        x1, h2t, r2, cnt, e1, e2, hist_sn, vn = _mixer(
            xs, mod_s, hist_s, *common, ws_s, bs_s, *tail,
            batch=dec_batch, seq=dec_seq, seqs=dec_batch, rows=dec_seq, chunk=dec_seq,
            pos0=past_len, emit_vn=True)
        xs = _peer(h2t, r2, cnt, e1, e2, u_b, vt_b, x1, gate2_s, g_final, tn=n_s,
                   tokens_per_gate=1, final_norm=final)

        pool_p.append(hist_p[:, 1:])
        pool_s.append(hist_sn[:, 1:])
        vn_s.append(vn.reshape(dec_batch, dec_seq, -1))

    return (xp.reshape(batch, seq, d), xs.reshape(dec_batch, dec_seq, d),
            jnp.stack(pool_p), jnp.stack(pool_s), jnp.stack(vn_s))
```

```python
import functools

import jax
import jax.numpy as jnp
from jax import lax
from jax.experimental import pallas as pl
from jax.experimental.pallas import tpu as pltpu

EPS = 1e-6
POOL_WINDOWS = (2, 4, 8, 16)
POOL_GROUP_DIM = 128
POOL_HIST_PAD = 16
GMLP_HEADS = 4
GMLP_HEAD_DIM = 128
GMLP_CHUNK = 128
PEER_HEADS = 8
PEER_NKEYS = 128
PEER_TOPK = 16
PEER_DHALF = 128
N_MOD = 6

LANES = 128
SUBLANES = 8
BF16_ROWS = 16
VMEM_LIMIT_BYTES = 56 * 1024 * 1024

MIXER_TOKENS = 256
PEER_TOKENS = 512
PEER_EXPERTS_TILE = 2048
MXU_ROWS = 128

_BF16 = jnp.bfloat16
_F32 = jnp.float32


def _gelu(x):
    return 0.5 * x * (1.0 + lax.erf(x * 0.7071067811865476))


def _rms(x, g):
    return x * lax.rsqrt(jnp.mean(x * x, axis=-1, keepdims=True) + EPS) * g


def _dot(a, b):
    return jnp.dot(a, b, preferred_element_type=_F32)


def _pack_pairs(x):
    return pltpu.bitcast(x.astype(_BF16), jnp.uint32)


def _unpack_pairs(x):
    return pltpu.bitcast(x, _BF16)


def _ada_kernel(c_ref, w_ref, b_ref, o_ref):
    c = c_ref[...]
    sc = c * jax.nn.sigmoid(c)
    o_ref[...] = _dot(sc.astype(_BF16), w_ref[...].astype(_BF16)) + b_ref[...]


def _ada(c, w, b):
    rows, d = c.shape
    n = w.shape[1]
    tile = 1536
    return pl.pallas_call(
        _ada_kernel,
        grid=(n // tile,),
        in_specs=[pl.BlockSpec((rows, d), lambda j: (0, 0)),
                  pl.BlockSpec((d, tile), lambda j: (0, j)),
                  pl.BlockSpec((1, tile), lambda j: (0, j))],
        out_specs=pl.BlockSpec((rows, tile), lambda j: (0, j)),
        out_shape=jax.ShapeDtypeStruct((rows, n), _F32),
        compiler_params=pltpu.CompilerParams(
            dimension_semantics=("arbitrary",), vmem_limit_bytes=VMEM_LIMIT_BYTES),
        name="adaln",
    )(c, w, b.reshape(1, n))


def _sort_network(n):
    def merge(lo, hi, r):
        step = r * 2
        if step < hi - lo:
            yield from merge(lo, hi, step)
            yield from merge(lo + r, hi, step)
            yield from ((i, i + r) for i in range(lo + r, hi - r, step))
        else:
            yield (lo, lo + r)

    def sort(lo, hi):
        if hi - lo >= 1:
            mid = lo + (hi - lo) // 2
            yield from sort(lo, mid)
            yield from sort(mid + 1, hi)
            yield from merge(lo, hi, 1)
    return tuple(sort(0, n - 1))


_SORT_TOPK = _sort_network(PEER_TOPK)
_BITONIC_TOPK = tuple((i, i + (PEER_TOPK >> k)) for k in range(1, PEER_TOPK.bit_length())
                      for i in range(PEER_TOPK) if not i & (PEER_TOPK >> k))


def _top16(xs):
    xs = list(xs)

    def exchange(net):
        for i, j in net:
            xs[i], xs[j] = jnp.maximum(xs[i], xs[j]), jnp.minimum(xs[i], xs[j])

    exchange(_SORT_TOPK)
    shift = 1
    while shift < SUBLANES:
        other = [pltpu.roll(x, shift, 0) for x in xs]
        xs = [jnp.maximum(x, o) for x, o in zip(xs, other[::-1])]
        exchange(_BITONIC_TOPK)
        shift *= 2
    return xs


def _mixer_kernel(x_ref, mod_ref, hist_ref, g1_ref, w_in_ref, pool_w_ref, pool_scale_ref,
                  gmlp_g_ref, ws_ref, bs_ref, w_out_ref, g2_ref, wqt_ref, keys_ref,
                  *rest, seqs, rows, chunk, pos0, carry, emit_vn):
    if emit_vn:
        (x1_ref, h2t_ref, r2_ref, cnt_ref, e1_ref, e2_ref, hist_out_ref, vn_ref,
         pbuf, qt_scr, s_scr, sv_scr) = rest
    else:
        (x1_ref, h2t_ref, r2_ref, cnt_ref, e1_ref, e2_ref, hist_out_ref,
         pbuf, qt_scr, s_scr, sv_scr) = rest
        vn_ref = None
    t = pl.program_id(1)
    tt = seqs * rows
    pool_width = pool_w_ref.shape[0] * POOL_GROUP_DIM
    gmlp_width = GMLP_HEADS * GMLP_HEAD_DIM

    x = x_ref[...]
    shift1, scale1, gate1, shift2, scale2 = (mod_ref[i] for i in range(5))
    h = _rms(x, g1_ref[...]) * (1.0 + scale1) + shift1
    z = _dot(h.astype(_BF16), w_in_ref[...])
    p = z[:, :pool_width]
    u = _gelu(z[:, pool_width:pool_width + gmlp_width])
    v = _gelu(z[:, pool_width + gmlp_width:])

    if carry:
        @pl.when(t == 0)
        def _():
            pbuf[:, 0:POOL_HIST_PAD, :] = hist_ref[...]
    else:
        pbuf[:, 0:POOL_HIST_PAD, :] = hist_ref[...]
    p3 = p.reshape(seqs, rows, pool_width)
    pbuf[:, POOL_HIST_PAD:POOL_HIST_PAD + rows, :] = p3
    pos = (pos0 + t * rows + lax.broadcasted_iota(jnp.int32, (1, rows, 1), 1)).astype(_F32)
    pool_parts = []
    for g, w in enumerate(POOL_WINDOWS):
        cols = slice(g * POOL_GROUP_DIM, (g + 1) * POOL_GROUP_DIM)
        tok = p3[:, :, cols]
        acc = tok
        for k in range(1, w):
            acc = acc + pbuf[:, POOL_HIST_PAD - k:POOL_HIST_PAD - k + rows, cols]
        cnt = jnp.minimum(pos + 1.0, float(w))
        diff = (acc / cnt - tok).reshape(tt, POOL_GROUP_DIM)
        pool_parts.append(_dot(diff.astype(_BF16), pool_w_ref[g]))
    pool_out = jnp.concatenate(pool_parts, axis=-1) * pool_scale_ref[...]
    hist_out_ref[...] = pbuf[:, rows:rows + POOL_HIST_PAD, :]
    if carry:
        pbuf[:, 0:POOL_HIST_PAD, :] = pbuf[:, rows:rows + POOL_HIST_PAD, :]

    cw = ws_ref.shape[1]
    ri = lax.broadcasted_iota(jnp.int32, (cw, cw), 0)
    ci = lax.broadcasted_iota(jnp.int32, (cw, cw), 1)
    causal = (ri // chunk == ci // chunk) & (ci % chunk <= ri % chunk)
    gm_parts = []
    vn_parts = []
    for hh in range(GMLP_HEADS):
        cols = slice(hh * GMLP_HEAD_DIM, (hh + 1) * GMLP_HEAD_DIM)
        vn = _rms(v[:, cols], gmlp_g_ref[:, cols])
        vn_parts.append(vn)
        wm = jnp.where(causal, ws_ref[hh], 0.0).astype(_BF16)
        vnb = vn.astype(_BF16)
        sp = [_dot(wm, vnb[c * cw:(c + 1) * cw]) + bs_ref[:, hh:hh + 1] for c in range(tt // cw)]
        s = sp[0] if len(sp) == 1 else jnp.concatenate(sp, axis=0)
        gm_parts.append(u[:, cols] * s)
    if vn_ref is not None:
        vn_ref[...] = jnp.concatenate(vn_parts, axis=-1)

    mix_in = jnp.concatenate([pool_out] + gm_parts, axis=-1)
    mix = _dot(mix_in.astype(_BF16), w_out_ref[...])
    x1 = x + gate1 * mix
    x1_ref[...] = x1

    h2 = _rms(x1, g2_ref[...]) * (1.0 + scale2) + shift2
    h2t = h2.T.astype(_BF16)
    h2t_ref[...] = h2t
    qt_scr[...] = _dot(wqt_ref[...], h2t)

    def score_body(hp, c):
        q = qt_scr[pl.ds(pl.multiple_of(hp * PEER_DHALF, PEER_DHALF), PEER_DHALF), :]
        s = _dot(keys_ref[hp], q.astype(_BF16))
        s_scr[hp] = s
        tops = _top16([s[g * SUBLANES:(g + 1) * SUBLANES] for g in range(PEER_NKEYS // SUBLANES)])
        for a in range(PEER_TOPK):
            sv_scr[hp, a:a + 1, :] = tops[a][0:1]
        return c
    lax.fori_loop(0, 2 * PEER_HEADS, score_body, 0, unroll=2)

    row8 = lax.broadcasted_iota(jnp.int32, (SUBLANES, tt), 0)
    row16 = lax.broadcasted_iota(jnp.int32, (PEER_TOPK, tt), 0)

    def by_chunk(ref, hd, val):
        for cc in range(tt // LANES):
            ref[hd, cc] = val[:, cc * LANES:(cc + 1) * LANES]

    def head_body(hd, c):
        sv1 = sv_scr[2 * hd]
        sv2 = sv_scr[2 * hd + 1]
        lo2 = sv2[0:SUBLANES]
        cands = [sv1[0:1] + lo2, sv1[0:1] + sv2[SUBLANES:]]
        for a in range(1, PEER_TOPK - 2):
            cands.append(jnp.where(row8 < PEER_TOPK // (a + 1), sv1[a:a + 1] + lo2, -jnp.inf))
        cands.append(jnp.where(row8 == 0, sv1[PEER_TOPK - 2:PEER_TOPK - 1] + sv2[0:1],
                               jnp.where(row8 == 1, sv1[PEER_TOPK - 1:] + sv2[0:1], -jnp.inf)))
        tops = _top16(cands)
        thr = tops[PEER_TOPK - 1][0:1]
        mx = tops[0][0:1]
        cand = jnp.concatenate(cands, axis=0)
        zsum = jnp.sum(jnp.where(cand >= thr, jnp.exp(cand - mx), 0.0), axis=0, keepdims=True)

        s1 = s_scr[2 * hd]
        s2 = s_scr[2 * hd + 1]
        cnt = jnp.zeros(s1.shape, _F32)
        above = jnp.zeros(s2.shape, _F32)
        for a in range(PEER_TOPK):
            nb = PEER_TOPK // (a + 1)
            part, rws = (sv2, row16) if nb > SUBLANES else (lo2, row8)
            ok = ((sv1[a:a + 1] + part) >= thr) & (rws < nb)
            n_sel = jnp.sum(jnp.where(ok, 1.0, 0.0), axis=0, keepdims=True)
            cnt = jnp.where(s1 == sv1[a:a + 1], n_sel, cnt)
            above = jnp.where(s2 < sv2[a:a + 1], float(a + 1), above)
        by_chunk(cnt_ref, hd, cnt)
        by_chunk(r2_ref, hd, _pack_pairs(above))
        by_chunk(e1_ref, hd, jnp.exp(s1 - sv1[0:1]))
        by_chunk(e2_ref, hd, _pack_pairs(jnp.exp(s2 - sv2[0:1]) / zsum))
        return c
    lax.fori_loop(0, PEER_HEADS, head_body, 0, unroll=2)


def _mixer(x, mod, hist, g1, w_in, pool_w, pool_scale, gmlp_g, ws, bs, w_out, g2, wqt, keys,
           *, batch, seq, seqs, rows, chunk, pos0, emit_vn):
    n, d = x.shape
    tt = seqs * rows
    steps = seq // rows
    carry = steps > 1
    nb = batch // seqs
    pool_width = pool_w.shape[0] * POOL_GROUP_DIM
    gmlp_width = GMLP_HEADS * GMLP_HEAD_DIM
    hp2 = 2 * PEER_HEADS
    mod_rows = mod.shape[1]
    tok = lambda b, t: (b * steps + t, 0)
    const2 = lambda b, t: (0, 0)
    const3 = lambda b, t: (0, 0, 0)
    if mod_rows == 1:
        mod_spec = pl.BlockSpec((N_MOD, 1, d), lambda b, t: (b, 0, 0))
    else:
        mod_spec = pl.BlockSpec((N_MOD, tt, d), lambda b, t: (0, b * steps + t, 0))
    in_specs = [
        pl.BlockSpec((tt, d), tok),
        mod_spec,
        pl.BlockSpec((seqs, POOL_HIST_PAD, pool_width), lambda b, t: (b, 0, 0)),
        pl.BlockSpec((1, d), const2),
        pl.BlockSpec(w_in.shape, const2),
        pl.BlockSpec(pool_w.shape, const3),
        pl.BlockSpec((1, pool_width), const2),
        pl.BlockSpec((1, gmlp_width), const2),
        pl.BlockSpec(ws.shape, const3),
        pl.BlockSpec(bs.shape, const2),
        pl.BlockSpec(w_out.shape, const2),
        pl.BlockSpec((1, d), const2),
        pl.BlockSpec(wqt.shape, const2),
        pl.BlockSpec(keys.shape, const3),
    ]
    def by_chunk(rows_, dt):
        spec = pl.BlockSpec((PEER_HEADS, tt // LANES, rows_, LANES),
                            lambda b, t: (0, b * steps + t, 0, 0))
        return spec, jax.ShapeDtypeStruct((PEER_HEADS, n // LANES, rows_, LANES), dt)
    (pair_spec, pair_shape), (key_spec, key_shape) = (
        by_chunk(PEER_NKEYS // 2, jnp.uint32), by_chunk(PEER_NKEYS, _F32))
    out_specs = [
        pl.BlockSpec((tt, d), tok),
        pl.BlockSpec((d, tt), lambda b, t: (0, b * steps + t)),
        pair_spec, key_spec, key_spec, pair_spec,
        pl.BlockSpec((seqs, POOL_HIST_PAD, pool_width), lambda b, t: (b, 0, 0)),
    ]
    out_shape = [
        jax.ShapeDtypeStruct((n, d), _F32),
        jax.ShapeDtypeStruct((d, n), _BF16),
        pair_shape, key_shape, key_shape, pair_shape,
        jax.ShapeDtypeStruct((batch, POOL_HIST_PAD, pool_width), _F32),
    ]
    if emit_vn:
        out_specs.append(pl.BlockSpec((tt, gmlp_width), tok))
        out_shape.append(jax.ShapeDtypeStruct((n, gmlp_width), _F32))
    body = functools.partial(_mixer_kernel, seqs=seqs, rows=rows, chunk=chunk, pos0=pos0,
                             carry=carry, emit_vn=emit_vn)
    return pl.pallas_call(
        body,
        grid=(nb, steps),
        in_specs=in_specs,
        out_specs=out_specs,
        out_shape=out_shape,
        scratch_shapes=[
            pltpu.VMEM((seqs, POOL_HIST_PAD + rows, pool_width), _F32),
            pltpu.VMEM((hp2 * PEER_DHALF, tt), _F32),
            pltpu.VMEM((hp2, PEER_NKEYS, tt), _F32),
            pltpu.VMEM((hp2, PEER_TOPK, tt), _F32),
        ],
        compiler_params=pltpu.CompilerParams(
            dimension_semantics=("arbitrary", "arbitrary"), vmem_limit_bytes=VMEM_LIMIT_BYTES),
        name="mixer_seq" if carry else "mixer_step",
    )(x, mod, hist, g1, w_in, pool_w, pool_scale, gmlp_g, ws, bs, w_out, g2, wqt, keys)


def _peer_kernel(h2t_ref, r2_ref, cnt_ref, e1_ref, e2_ref, u_ref, vt_ref, x1_ref,
                 gate_ref, gf_ref, y_ref, a_scr, w_scr, acc_scr, *, n_k, final_norm):
    s = pl.program_id(0)
    _, te, tn = a_scr.shape
    d = acc_scr.shape[0]
    n_i1 = te // PEER_NKEYS
    k_red = jnp.maximum(s - 2, 0) % n_k

    @pl.when(s == 0)
    def _():
        a_scr[...] = jnp.zeros_like(a_scr)
        w_scr[...] = jnp.zeros_like(w_scr)

    def step(cur):
        prev = 1 - cur

        def score_rows(m):
            rws = slice(m * MXU_ROWS, (m + 1) * MXU_ROWS)
            a_scr[cur, rws, :] = _dot(u_ref[rws, :], h2t_ref[...])

        def reduce_rows(m):
            rws = slice(m * MXU_ROWS, (m + 1) * MXU_ROWS)
            contrib = _dot(vt_ref[rws, :], w_scr[cur])
            acc_scr[rws, :] = contrib + jnp.where(k_red > 0, acc_scr[rws, :], 0.0)

        mxu_work = ([functools.partial(score_rows, m) for m in range(te // MXU_ROWS)]
                    + [functools.partial(reduce_rows, m) for m in range(d // MXU_ROWS)])

        def all_rows(row):
            tile = jnp.broadcast_to(row, (BF16_ROWS, LANES)).astype(_BF16)
            return jnp.broadcast_to(tile[None], (PEER_NKEYS // BF16_ROWS, BF16_ROWS, LANES)
                                    ).reshape(PEER_NKEYS, LANES)

        def gate_block(cc, j):
            lanes = slice(cc * LANES, (cc + 1) * LANES)
            rws = slice(j * PEER_NKEYS, (j + 1) * PEER_NKEYS)
            gate = None
            for hd in range(PEER_HEADS):
                sel = _unpack_pairs(r2_ref[hd, cc]) < all_rows(cnt_ref[hd, cc, j:j + 1, :])
                term = jnp.where(
                    sel,
                    _unpack_pairs(e2_ref[hd, cc]) * all_rows(e1_ref[hd, cc, j:j + 1, :]),
                    jnp.zeros((), _BF16))
                gate = term if gate is None else gate + term
            act = _gelu(a_scr[prev, rws, lanes])
            w_scr[prev, rws, lanes] = gate * act.astype(_BF16)

        vpu_work = [functools.partial(gate_block, cc, j)
                    for cc in range(tn // LANES) for j in range(n_i1)]

        per_piece = -(-len(vpu_work) // len(mxu_work))
        for i, piece in enumerate(mxu_work):
            piece()
            for blk in vpu_work[i * per_piece:(i + 1) * per_piece]:
                blk()

    for parity in range(2):
        pl.when(s % 2 == parity)(functools.partial(step, parity))

    @pl.when((s >= 2) & (k_red == n_k - 1))
    def _():
        x = x1_ref[...] + gate_ref[...] * acc_scr[...].T
        y_ref[...] = _rms(x, gf_ref[...]) if final_norm else x


def _peer(h2t, r2, cnt, e1, e2, u_tab, vt_tab, x1, gate2, g_final, *, tn, tokens_per_gate,
          final_norm):
    d, n = h2t.shape
    n_exp = u_tab.shape[0]
    te = PEER_EXPERTS_TILE
    n_i1 = te // PEER_NKEYS
    n_k = n_exp // te
    pairs = (n // tn) * n_k

    def stage(lag):
        def split(s):
            p = jnp.clip(s - lag, 0, pairs - 1)
            return p // n_k, p % n_k
        return split
    score, gating, reduce_ = stage(0), stage(1), stage(2)

    gate_rows = gate2.shape[0]
    if gate_rows == n:
        gate_spec = pl.BlockSpec((tn, d), lambda s: (reduce_(s)[0], 0))
    else:
        gate_spec = pl.BlockSpec(
            (None, 1, d), lambda s: ((reduce_(s)[0] * tn) // tokens_per_gate, 0, 0))
        gate2 = gate2.reshape(gate_rows, 1, d)
    per_i1 = pl.BlockSpec((PEER_HEADS, tn // LANES, n_i1, LANES),
                          lambda s: (0, gating(s)[0], gating(s)[1], 0))
    per_i2 = pl.BlockSpec((PEER_HEADS, tn // LANES, PEER_NKEYS // 2, LANES),
                          lambda s: (0, gating(s)[0], 0, 0))
    return pl.pallas_call(
        functools.partial(_peer_kernel, n_k=n_k, final_norm=final_norm),
        grid=(pairs + 2,),
        in_specs=[
            pl.BlockSpec((d, tn), lambda s: (0, score(s)[0])),
            per_i2, per_i1, per_i1, per_i2,
            pl.BlockSpec((te, d), lambda s: (score(s)[1], 0)),
            pl.BlockSpec((d, te), lambda s: (0, reduce_(s)[1])),
            pl.BlockSpec((tn, d), lambda s: (reduce_(s)[0], 0)),
            gate_spec,
            pl.BlockSpec((1, d), lambda s: (0, 0)),
        ],
        out_specs=pl.BlockSpec((tn, d), lambda s: (reduce_(s)[0], 0)),
        out_shape=jax.ShapeDtypeStruct((n, d), _F32),
        scratch_shapes=[
            pltpu.VMEM((2, te, tn), _F32),
            pltpu.VMEM((2, te, tn), _BF16),
            pltpu.VMEM((d, tn), _F32),
        ],
        compiler_params=pltpu.CompilerParams(
            dimension_semantics=("arbitrary",), vmem_limit_bytes=VMEM_LIMIT_BYTES),
        name="peer_prompt" if tn == PEER_TOKENS else "peer_step",
    )(h2t, r2, cnt, e1, e2, u_tab, vt_tab, x1, gate2, g_final.reshape(1, d))


def kernel(x_prompt, x_sample, cache_pool, c_prompt, c_sample, w_ada, b_ada, g_norm1, w_in,
           pool_w, pool_scale, gmlp_norm_g, gmlp_ws, gmlp_bs, w_out, g_norm2, peer_wq,
           peer_keys, peer_u, peer_v, g_final):
    depth = w_ada.shape[0]
    batch, seq, d = x_prompt.shape
    dec_batch, dec_seq, _ = x_sample.shape
    past_len = 2048
    n_p, n_s = batch * seq, dec_batch * dec_seq
    pool_width = cache_pool.shape[-1]
    assert seq % MIXER_TOKENS == 0 and MIXER_TOKENS % GMLP_CHUNK == 0
    assert dec_seq % SUBLANES == 0 and dec_seq <= GMLP_CHUNK and POOL_HIST_PAD <= dec_seq
    assert n_p % PEER_TOKENS == 0 and seq % PEER_TOKENS == 0

    xp = x_prompt.reshape(n_p, d)
    xs = x_sample.reshape(n_s, d)
    c_rows = batch + dec_batch
    c_pad = -(-c_rows // SUBLANES) * SUBLANES
    c_all = jnp.concatenate([c_prompt, c_sample, jnp.zeros((c_pad - c_rows, d), _F32)], axis=0)
    reps = n_s // dec_seq

    pool_p, pool_s, vn_s = [], [], []
    for l in range(depth):
        mod = _ada(c_all, w_ada[l], b_ada[l]).reshape(c_pad, N_MOD, d)
        mod_p = mod[:batch].reshape(batch * N_MOD, 1, d)
        mod_s = jnp.repeat(mod[batch:c_rows].transpose(1, 0, 2), dec_seq, axis=1)
        gate2_p = mod[:batch, 5]
        gate2_s = mod_s[5]

        w_in_b = w_in[l].astype(_BF16)
        pool_w_b = pool_w[l].astype(_BF16)
        w_out_b = w_out[l].astype(_BF16)
        wqt_b = peer_wq[l].T.astype(_BF16)
        keys_b = peer_keys[l].reshape(2 * PEER_HEADS, PEER_NKEYS, PEER_DHALF).astype(_BF16)
        u_b = peer_u[l].astype(_BF16)
        vt_b = peer_v[l].T.astype(_BF16)
        g1 = g_norm1[l].reshape(1, d)
        g2 = g_norm2[l].reshape(1, d)
        pscale = pool_scale[l].reshape(1, pool_width)
        gg = gmlp_norm_g[l].reshape(1, -1)
        final = l == depth - 1

        common = (g1, w_in_b, pool_w_b, pscale, gg)
        tail = (w_out_b, g2, wqt_b, keys_b)

        hist0 = jnp.zeros((batch, POOL_HIST_PAD, pool_width), _F32)
        x1, h2t, r2, cnt, e1, e2, hist_p = _mixer(
            xp, mod_p, hist0, *common, gmlp_ws[l], gmlp_bs[l].T, *tail,
            batch=batch, seq=seq, seqs=1, rows=MIXER_TOKENS, chunk=GMLP_CHUNK, pos0=0,
            emit_vn=False)
        xp = _peer(h2t, r2, cnt, e1, e2, u_b, vt_b, x1, gate2_p, g_final, tn=PEER_TOKENS,
                   tokens_per_gate=seq, final_norm=final)

        hist_s = jnp.pad(cache_pool[l], ((0, 0), (POOL_HIST_PAD - cache_pool.shape[2], 0), (0, 0)))
        ws_s = jnp.tile(gmlp_ws[l][:, :dec_seq, :dec_seq], (1, reps, reps))
        bs_s = jnp.tile(gmlp_bs[l][:, :dec_seq], (1, reps)).T
        x1, h2t, r2, cnt, e1, e2, hist_sn, vn = _mixer(
            xs, mod_s, hist_s, *common, ws_s, bs_s, *tail,
            batch=dec_batch, seq=dec_seq, seqs=dec_batch, rows=dec_seq, chunk=dec_seq,
            pos0=past_len, emit_vn=True)
        xs = _peer(h2t, r2, cnt, e1, e2, u_b, vt_b, x1, gate2_s, g_final, tn=n_s,
                   tokens_per_gate=1, final_norm=final)

        pool_p.append(hist_p[:, 1:])
        pool_s.append(hist_sn[:, 1:])
        vn_s.append(vn.reshape(dec_batch, dec_seq, -1))

    return (xp.reshape(batch, seq, d), xs.reshape(dec_batch, dec_seq, d),
            jnp.stack(pool_p), jnp.stack(pool_s), jnp.stack(vn_s))
```

```python
import functools

import jax
import jax.numpy as jnp
from jax import lax
from jax.experimental import pallas as pl
from jax.experimental.pallas import tpu as pltpu

EPS = 1e-6
POOL_WINDOWS = (2, 4, 8, 16)
POOL_GROUP_DIM = 128
POOL_HIST_PAD = 16
GMLP_HEADS = 4
GMLP_HEAD_DIM = 128
GMLP_CHUNK = 128
PEER_HEADS = 8
PEER_NKEYS = 128
PEER_TOPK = 16
PEER_DHALF = 128
N_MOD = 6

LANES = 128
SUBLANES = 8
BF16_ROWS = 16
VMEM_LIMIT_BYTES = 56 * 1024 * 1024

MIXER_TOKENS = 256
PEER_TOKENS = 512
PEER_EXPERTS_TILE = 2048
MXU_ROWS = 128
MXU_COLS = 256
GATE_I1 = 2

_BF16 = jnp.bfloat16
_F32 = jnp.float32


def _gelu(x):
    return 0.5 * x * (1.0 + lax.erf(x * 0.7071067811865476))


def _rms(x, g):
    return x * lax.rsqrt(jnp.mean(x * x, axis=-1, keepdims=True) + EPS) * g


def _dot(a, b):
    return jnp.dot(a, b, preferred_element_type=_F32)


def _pack_pairs(x):
    return pltpu.bitcast(x.astype(_BF16), jnp.uint32)


def _unpack_pairs(x):
    return pltpu.bitcast(x, _BF16)


def _twin_words(x):
    hi = lax.bitcast_convert_type(x.astype(_BF16).astype(_F32), jnp.uint32)
    return hi | (hi >> 16)


def _ada_kernel(c_ref, w_ref, b_ref, o_ref):
    c = c_ref[...]
    sc = c * jax.nn.sigmoid(c)
    o_ref[...] = _dot(sc.astype(_BF16), w_ref[...].astype(_BF16)) + b_ref[...]


def _ada(c, w, b):
    rows, d = c.shape
    n = w.shape[1]
    tile = 1536
    return pl.pallas_call(
        _ada_kernel,
        grid=(n // tile,),
        in_specs=[pl.BlockSpec((rows, d), lambda j: (0, 0)),
                  pl.BlockSpec((d, tile), lambda j: (0, j)),
                  pl.BlockSpec((1, tile), lambda j: (0, j))],
        out_specs=pl.BlockSpec((rows, tile), lambda j: (0, j)),
        out_shape=jax.ShapeDtypeStruct((rows, n), _F32),
        compiler_params=pltpu.CompilerParams(
            dimension_semantics=("arbitrary",), vmem_limit_bytes=VMEM_LIMIT_BYTES),
        name="adaln",
    )(c, w, b.reshape(1, n))


def _sort_network(n):
    def merge(lo, hi, r):
        step = r * 2
        if step < hi - lo:
            yield from merge(lo, hi, step)
            yield from merge(lo + r, hi, step)
            yield from ((i, i + r) for i in range(lo + r, hi - r, step))
        else:
            yield (lo, lo + r)

    def sort(lo, hi):
        if hi - lo >= 1:
            mid = lo + (hi - lo) // 2
            yield from sort(lo, mid)
            yield from sort(mid + 1, hi)
            yield from merge(lo, hi, 1)
    return tuple(sort(0, n - 1))


_SORT_TOPK = _sort_network(PEER_TOPK)
_BITONIC_TOPK = tuple((i, i + (PEER_TOPK >> k)) for k in range(1, PEER_TOPK.bit_length())
                      for i in range(PEER_TOPK) if not i & (PEER_TOPK >> k))


def _top16(xs):
    xs = list(xs)

    def exchange(net):
        for i, j in net:
            xs[i], xs[j] = jnp.maximum(xs[i], xs[j]), jnp.minimum(xs[i], xs[j])

    exchange(_SORT_TOPK)
    shift = 1
    while shift < SUBLANES:
        other = [pltpu.roll(x, shift, 0) for x in xs]
        xs = [jnp.maximum(x, o) for x, o in zip(xs, other[::-1])]
        exchange(_BITONIC_TOPK)
        shift *= 2
    return xs


def _mixer_kernel(x_ref, mod_ref, hist_ref, g1_ref, w_in_ref, pool_w_ref, pool_scale_ref,
                  gmlp_g_ref, ws_ref, bs_ref, w_out_ref, g2_ref, wqt_ref, keys_ref,
                  *rest, seqs, rows, chunk, pos0, carry, emit_vn):
    if emit_vn:
        (x1_ref, h2t_ref, r2_ref, cnt_ref, e1_ref, e2_ref, hist_out_ref, vn_ref,
         pbuf, qt_scr, s_scr, sv_scr) = rest
    else:
        (x1_ref, h2t_ref, r2_ref, cnt_ref, e1_ref, e2_ref, hist_out_ref,
         pbuf, qt_scr, s_scr, sv_scr) = rest
        vn_ref = None
    t = pl.program_id(1)
    tt = seqs * rows
    pool_width = pool_w_ref.shape[0] * POOL_GROUP_DIM
    gmlp_width = GMLP_HEADS * GMLP_HEAD_DIM

    x = x_ref[...]
    shift1, scale1, gate1, shift2, scale2 = (mod_ref[i] for i in range(5))
    h = _rms(x, g1_ref[...]) * (1.0 + scale1) + shift1
    z = _dot(h.astype(_BF16), w_in_ref[...])
    p = z[:, :pool_width]
    u = _gelu(z[:, pool_width:pool_width + gmlp_width])
    v = _gelu(z[:, pool_width + gmlp_width:])

    if carry:
        @pl.when(t == 0)
        def _():
            pbuf[:, 0:POOL_HIST_PAD, :] = hist_ref[...]
    else:
        pbuf[:, 0:POOL_HIST_PAD, :] = hist_ref[...]
    p3 = p.reshape(seqs, rows, pool_width)
    pbuf[:, POOL_HIST_PAD:POOL_HIST_PAD + rows, :] = p3
    pos = (pos0 + t * rows + lax.broadcasted_iota(jnp.int32, (1, rows, 1), 1)).astype(_F32)
    pool_parts = []
    for g, w in enumerate(POOL_WINDOWS):
        cols = slice(g * POOL_GROUP_DIM, (g + 1) * POOL_GROUP_DIM)
        tok = p3[:, :, cols]
        acc = tok
        for k in range(1, w):
            acc = acc + pbuf[:, POOL_HIST_PAD - k:POOL_HIST_PAD - k + rows, cols]
        cnt = jnp.minimum(pos + 1.0, float(w))
        diff = (acc / cnt - tok).reshape(tt, POOL_GROUP_DIM)
        pool_parts.append(_dot(diff.astype(_BF16), pool_w_ref[g]))
    pool_out = jnp.concatenate(pool_parts, axis=-1) * pool_scale_ref[...]
    hist_out_ref[...] = pbuf[:, rows:rows + POOL_HIST_PAD, :]
    if carry:
        pbuf[:, 0:POOL_HIST_PAD, :] = pbuf[:, rows:rows + POOL_HIST_PAD, :]

    cw = ws_ref.shape[1]
    ri = lax.broadcasted_iota(jnp.int32, (cw, cw), 0)
    ci = lax.broadcasted_iota(jnp.int32, (cw, cw), 1)
    causal = (ri // chunk == ci // chunk) & (ci % chunk <= ri % chunk)
    gm_parts = []
    vn_parts = []
    for hh in range(GMLP_HEADS):
        cols = slice(hh * GMLP_HEAD_DIM, (hh + 1) * GMLP_HEAD_DIM)
        vn = _rms(v[:, cols], gmlp_g_ref[:, cols])
        vn_parts.append(vn)
        wm = jnp.where(causal, ws_ref[hh], 0.0).astype(_BF16)
        vnb = vn.astype(_BF16)
        sp = [_dot(wm, vnb[c * cw:(c + 1) * cw]) + bs_ref[:, hh:hh + 1] for c in range(tt // cw)]
        s = sp[0] if len(sp) == 1 else jnp.concatenate(sp, axis=0)
        gm_parts.append(u[:, cols] * s)
    if vn_ref is not None:
        vn_ref[...] = jnp.concatenate(vn_parts, axis=-1)

    mix_in = jnp.concatenate([pool_out] + gm_parts, axis=-1)
    mix = _dot(mix_in.astype(_BF16), w_out_ref[...])
    x1 = x + gate1 * mix
    x1_ref[...] = x1

    h2 = _rms(x1, g2_ref[...]) * (1.0 + scale2) + shift2
    h2t = h2.T.astype(_BF16)
    h2t_ref[...] = h2t
    qt_scr[...] = _dot(wqt_ref[...], h2t)

    def score_body(hp, c):
        q = qt_scr[pl.ds(pl.multiple_of(hp * PEER_DHALF, PEER_DHALF), PEER_DHALF), :]
        s = _dot(keys_ref[hp], q.astype(_BF16))
        s_scr[hp] = s
        tops = _top16([s[g * SUBLANES:(g + 1) * SUBLANES] for g in range(PEER_NKEYS // SUBLANES)])
        for a in range(PEER_TOPK):
            sv_scr[hp, a:a + 1, :] = tops[a][0:1]
        return c
    lax.fori_loop(0, 2 * PEER_HEADS, score_body, 0, unroll=4)

    row8 = lax.broadcasted_iota(jnp.int32, (SUBLANES, tt), 0)
    row16 = lax.broadcasted_iota(jnp.int32, (PEER_TOPK, tt), 0)

    def by_chunk(ref, hd, val):
        for cc in range(tt // LANES):
            ref[hd, cc] = val[:, cc * LANES:(cc + 1) * LANES]

    def head_body(hd, c):
        sv1 = sv_scr[2 * hd]
        sv2 = sv_scr[2 * hd + 1]
        lo2 = sv2[0:SUBLANES]
        cands = [sv1[0:1] + lo2, sv1[0:1] + sv2[SUBLANES:]]
        for a in range(1, PEER_TOPK - 2):
            cands.append(jnp.where(row8 < PEER_TOPK // (a + 1), sv1[a:a + 1] + lo2, -jnp.inf))
        cands.append(jnp.where(row8 == 0, sv1[PEER_TOPK - 2:PEER_TOPK - 1] + sv2[0:1],
                               jnp.where(row8 == 1, sv1[PEER_TOPK - 1:] + sv2[0:1], -jnp.inf)))
        tops = _top16(cands)
        thr = tops[PEER_TOPK - 1][0:1]
        mx = tops[0][0:1]
        cand = jnp.concatenate(cands, axis=0)
        zsum = jnp.sum(jnp.where(cand >= thr, jnp.exp(cand - mx), 0.0), axis=0, keepdims=True)

        s1 = s_scr[2 * hd]
        s2 = s_scr[2 * hd + 1]
        cnt = jnp.zeros(s1.shape, jnp.uint32)
        above = jnp.zeros(s2.shape, _F32)
        for a in range(PEER_TOPK):
            nb = PEER_TOPK // (a + 1)
            part, rws = (sv2, row16) if nb > SUBLANES else (lo2, row8)
            ok = ((sv1[a:a + 1] + part) >= thr) & (rws < nb)
            n_sel = _twin_words(jnp.sum(jnp.where(ok, 1.0, 0.0), axis=0, keepdims=True))
            cnt = jnp.where(s1 == sv1[a:a + 1], n_sel, cnt)
            above = jnp.where(s2 < sv2[a:a + 1], float(a + 1), above)
        by_chunk(cnt_ref, hd, cnt)
        by_chunk(r2_ref, hd, _pack_pairs(above))
        by_chunk(e1_ref, hd, _twin_words(jnp.exp(s1 - sv1[0:1])))
        by_chunk(e2_ref, hd, _pack_pairs(jnp.exp(s2 - sv2[0:1]) / zsum))
        return c
    lax.fori_loop(0, PEER_HEADS, head_body, 0, unroll=2)


def _mixer(x, mod, hist, g1, w_in, pool_w, pool_scale, gmlp_g, ws, bs, w_out, g2, wqt, keys,
           *, batch, seq, seqs, rows, chunk, pos0, emit_vn):
    n, d = x.shape
    tt = seqs * rows
    steps = seq // rows
    carry = steps > 1
    nb = batch // seqs
    pool_width = pool_w.shape[0] * POOL_GROUP_DIM
    gmlp_width = GMLP_HEADS * GMLP_HEAD_DIM
    hp2 = 2 * PEER_HEADS
    mod_rows = mod.shape[1]
    tok = lambda b, t: (b * steps + t, 0)
    const2 = lambda b, t: (0, 0)
    const3 = lambda b, t: (0, 0, 0)
    if mod_rows == 1:
        mod_spec = pl.BlockSpec((N_MOD, 1, d), lambda b, t: (b, 0, 0))
    else:
        mod_spec = pl.BlockSpec((N_MOD, tt, d), lambda b, t: (0, b * steps + t, 0))
    in_specs = [
        pl.BlockSpec((tt, d), tok),
        mod_spec,
        pl.BlockSpec((seqs, POOL_HIST_PAD, pool_width), lambda b, t: (b, 0, 0)),
        pl.BlockSpec((1, d), const2),
        pl.BlockSpec(w_in.shape, const2),
        pl.BlockSpec(pool_w.shape, const3),
        pl.BlockSpec((1, pool_width), const2),
        pl.BlockSpec((1, gmlp_width), const2),
        pl.BlockSpec(ws.shape, const3),
        pl.BlockSpec(bs.shape, const2),
        pl.BlockSpec(w_out.shape, const2),
        pl.BlockSpec((1, d), const2),
        pl.BlockSpec(wqt.shape, const2),
        pl.BlockSpec(keys.shape, const3),
    ]
    def by_chunk(rows_, dt):
        spec = pl.BlockSpec((PEER_HEADS, tt // LANES, rows_, LANES),
                            lambda b, t: (0, b * steps + t, 0, 0))
        return spec, jax.ShapeDtypeStruct((PEER_HEADS, n // LANES, rows_, LANES), dt)
    (pair_spec, pair_shape), (key_spec, key_shape) = (
        by_chunk(PEER_NKEYS // 2, jnp.uint32), by_chunk(PEER_NKEYS, jnp.uint32))
    out_specs = [
        pl.BlockSpec((tt, d), tok),
        pl.BlockSpec((d, tt), lambda b, t: (0, b * steps + t)),
        pair_spec, key_spec, key_spec, pair_spec,
        pl.BlockSpec((seqs, POOL_HIST_PAD, pool_width), lambda b, t: (b, 0, 0)),
    ]
    out_shape = [
        jax.ShapeDtypeStruct((n, d), _F32),
        jax.ShapeDtypeStruct((d, n), _BF16),
        pair_shape, key_shape, key_shape, pair_shape,
        jax.ShapeDtypeStruct((batch, POOL_HIST_PAD, pool_width), _F32),
    ]
    if emit_vn:
        out_specs.append(pl.BlockSpec((tt, gmlp_width), tok))
        out_shape.append(jax.ShapeDtypeStruct((n, gmlp_width), _F32))
    body = functools.partial(_mixer_kernel, seqs=seqs, rows=rows, chunk=chunk, pos0=pos0,
                             carry=carry, emit_vn=emit_vn)
    return pl.pallas_call(
        body,
        grid=(nb, steps),
        in_specs=in_specs,
        out_specs=out_specs,
        out_shape=out_shape,
        scratch_shapes=[
            pltpu.VMEM((seqs, POOL_HIST_PAD + rows, pool_width), _F32),
            pltpu.VMEM((hp2 * PEER_DHALF, tt), _F32),
            pltpu.VMEM((hp2, PEER_NKEYS, tt), _F32),
            pltpu.VMEM((hp2, PEER_TOPK, tt), _F32),
        ],
        compiler_params=pltpu.CompilerParams(
            dimension_semantics=("arbitrary", "arbitrary"), vmem_limit_bytes=VMEM_LIMIT_BYTES),
        name="mixer_seq" if carry else "mixer_step",
    )(x, mod, hist, g1, w_in, pool_w, pool_scale, gmlp_g, ws, bs, w_out, g2, wqt, keys)


def _peer_kernel(h2t_ref, r2_ref, cnt_ref, e1_ref, e2_ref, u_ref, vt_ref, x1_ref,
                 gate_ref, gf_ref, y_ref, a_scr, w_scr, acc_scr, *, n_k, final_norm):
    s = pl.program_id(0)
    _, te, tn = a_scr.shape
    d = acc_scr.shape[0]
    n_i1 = te // PEER_NKEYS
    k_red = jnp.maximum(s - 2, 0) % n_k

    @pl.when(s == 0)
    def _():
        a_scr[...] = jnp.zeros_like(a_scr)
        w_scr[...] = jnp.zeros_like(w_scr)

    def step(cur):
        prev = 1 - cur
        ncol = min(MXU_COLS, tn)

        def score_piece(m, c):
            rws = slice(m * MXU_ROWS, (m + 1) * MXU_ROWS)
            cols = slice(c * ncol, (c + 1) * ncol)
            a_scr[cur, rws, cols] = _dot(u_ref[rws, :], h2t_ref[:, cols])

        def reduce_piece(m, c):
            rws = slice(m * MXU_ROWS, (m + 1) * MXU_ROWS)
            cols = slice(c * ncol, (c + 1) * ncol)
            contrib = _dot(vt_ref[rws, :], w_scr[cur, :, cols])
            acc_scr[rws, cols] = contrib + jnp.where(k_red > 0, acc_scr[rws, cols], 0.0)

        mxu_work = ([functools.partial(score_piece, m, c)
                     for m in range(te // MXU_ROWS) for c in range(tn // ncol)]
                    + [functools.partial(reduce_piece, m, c)
                       for m in range(d // MXU_ROWS) for c in range(tn // ncol)])

        def all_rows(row):
            tile = _unpack_pairs(jnp.broadcast_to(row, (SUBLANES, LANES)))
            return jnp.broadcast_to(tile[None], (PEER_NKEYS // BF16_ROWS, BF16_ROWS, LANES)
                                    ).reshape(PEER_NKEYS, LANES)

        def gate_block(cc, j0):
            lanes = slice(cc * LANES, (cc + 1) * LANES)
            gates = [None] * GATE_I1
            for hd in range(PEER_HEADS):
                r2 = _unpack_pairs(r2_ref[hd, cc])
                e2 = _unpack_pairs(e2_ref[hd, cc])
                for jj in range(GATE_I1):
                    j = j0 + jj
                    sel = r2 < all_rows(cnt_ref[hd, cc, j:j + 1, :])
                    term = jnp.where(sel, e2 * all_rows(e1_ref[hd, cc, j:j + 1, :]),
                                     jnp.zeros((), _BF16))
                    gates[jj] = term if gates[jj] is None else gates[jj] + term
            for jj in range(GATE_I1):
                rws = slice((j0 + jj) * PEER_NKEYS, (j0 + jj + 1) * PEER_NKEYS)
                act = _gelu(a_scr[prev, rws, lanes])
                w_scr[prev, rws, lanes] = gates[jj] * act.astype(_BF16)

        vpu_work = [functools.partial(gate_block, cc, j0)
                    for cc in range(tn // LANES) for j0 in range(0, n_i1, GATE_I1)]

        n_m, n_v = len(mxu_work), len(vpu_work)
        for i, piece in enumerate(mxu_work):
            piece()
            for blk in vpu_work[i * n_v // n_m:(i + 1) * n_v // n_m]:
                blk()

    for parity in range(2):
        pl.when(s % 2 == parity)(functools.partial(step, parity))

    @pl.when((s >= 2) & (k_red == n_k - 1))
    def _():
        x = x1_ref[...] + gate_ref[...] * acc_scr[...].T
        y_ref[...] = _rms(x, gf_ref[...]) if final_norm else x


def _peer(h2t, r2, cnt, e1, e2, u_tab, vt_tab, x1, gate2, g_final, *, tn, tokens_per_gate,
          final_norm):
    d, n = h2t.shape
    n_exp = u_tab.shape[0]
    te = PEER_EXPERTS_TILE
    n_i1 = te // PEER_NKEYS
    n_k = n_exp // te
    pairs = (n // tn) * n_k

    def stage(lag):
        def split(s):
            p = jnp.clip(s - lag, 0, pairs - 1)
            return p // n_k, p % n_k
        return split
    score, gating, reduce_ = stage(0), stage(1), stage(2)

    gate_rows = gate2.shape[0]
    if gate_rows == n:
        gate_spec = pl.BlockSpec((tn, d), lambda s: (reduce_(s)[0], 0))
    else:
        gate_spec = pl.BlockSpec(
            (None, 1, d), lambda s: ((reduce_(s)[0] * tn) // tokens_per_gate, 0, 0))
        gate2 = gate2.reshape(gate_rows, 1, d)
    per_i1 = pl.BlockSpec((PEER_HEADS, tn // LANES, n_i1, LANES),
                          lambda s: (0, gating(s)[0], gating(s)[1], 0))
    per_i2 = pl.BlockSpec((PEER_HEADS, tn // LANES, PEER_NKEYS // 2, LANES),
                          lambda s: (0, gating(s)[0], 0, 0))
    return pl.pallas_call(
        functools.partial(_peer_kernel, n_k=n_k, final_norm=final_norm),
        grid=(pairs + 2,),
        in_specs=[
            pl.BlockSpec((d, tn), lambda s: (0, score(s)[0])),
            per_i2, per_i1, per_i1, per_i2,
            pl.BlockSpec((te, d), lambda s: (score(s)[1], 0)),
            pl.BlockSpec((d, te), lambda s: (0, reduce_(s)[1])),
            pl.BlockSpec((tn, d), lambda s: (reduce_(s)[0], 0)),
            gate_spec,
            pl.BlockSpec((1, d), lambda s: (0, 0)),
        ],
        out_specs=pl.BlockSpec((tn, d), lambda s: (reduce_(s)[0], 0)),
        out_shape=jax.ShapeDtypeStruct((n, d), _F32),
        scratch_shapes=[
            pltpu.VMEM((2, te, tn), _F32),
            pltpu.VMEM((2, te, tn), _BF16),
            pltpu.VMEM((d, tn), _F32),
        ],
        compiler_params=pltpu.CompilerParams(
            dimension_semantics=("arbitrary",), vmem_limit_bytes=VMEM_LIMIT_BYTES),
        name="peer_prompt" if tn == PEER_TOKENS else "peer_step",
    )(h2t, r2, cnt, e1, e2, u_tab, vt_tab, x1, gate2, g_final.reshape(1, d))


def kernel(x_prompt, x_sample, cache_pool, c_prompt, c_sample, w_ada, b_ada, g_norm1, w_in,
           pool_w, pool_scale, gmlp_norm_g, gmlp_ws, gmlp_bs, w_out, g_norm2, peer_wq,
           peer_keys, peer_u, peer_v, g_final):
    depth = w_ada.shape[0]
    batch, seq, d = x_prompt.shape
    dec_batch, dec_seq, _ = x_sample.shape
    past_len = 2048
    n_p, n_s = batch * seq, dec_batch * dec_seq
    pool_width = cache_pool.shape[-1]
    assert seq % MIXER_TOKENS == 0 and MIXER_TOKENS % GMLP_CHUNK == 0
    assert dec_seq % SUBLANES == 0 and dec_seq <= GMLP_CHUNK and POOL_HIST_PAD <= dec_seq
    assert n_p % PEER_TOKENS == 0 and seq % PEER_TOKENS == 0

    xp = x_prompt.reshape(n_p, d)
    xs = x_sample.reshape(n_s, d)
    c_rows = batch + dec_batch
    c_pad = -(-c_rows // SUBLANES) * SUBLANES
    c_all = jnp.concatenate([c_prompt, c_sample, jnp.zeros((c_pad - c_rows, d), _F32)], axis=0)
    reps = n_s // dec_seq

    pool_p, pool_s, vn_s = [], [], []
    for l in range(depth):
        mod = _ada(c_all, w_ada[l], b_ada[l]).reshape(c_pad, N_MOD, d)
        mod_p = mod[:batch].reshape(batch * N_MOD, 1, d)
        mod_s = jnp.repeat(mod[batch:c_rows].transpose(1, 0, 2), dec_seq, axis=1)
        gate2_p = mod[:batch, 5]
        gate2_s = mod_s[5]

        w_in_b = w_in[l].astype(_BF16)
        pool_w_b = pool_w[l].astype(_BF16)
        w_out_b = w_out[l].astype(_BF16)
        wqt_b = peer_wq[l].T.astype(_BF16)
        keys_b = peer_keys[l].reshape(2 * PEER_HEADS, PEER_NKEYS, PEER_DHALF).astype(_BF16)
        u_b = peer_u[l].astype(_BF16)
        vt_b = peer_v[l].T.astype(_BF16)
        g1 = g_norm1[l].reshape(1, d)
        g2 = g_norm2[l].reshape(1, d)
        pscale = pool_scale[l].reshape(1, pool_width)
        gg = gmlp_norm_g[l].reshape(1, -1)
        final = l == depth - 1

        common = (g1, w_in_b, pool_w_b, pscale, gg)
        tail = (w_out_b, g2, wqt_b, keys_b)

        hist0 = jnp.zeros((batch, POOL_HIST_PAD, pool_width), _F32)
        x1, h2t, r2, cnt, e1, e2, hist_p = _mixer(
            xp, mod_p, hist0, *common, gmlp_ws[l], gmlp_bs[l].T, *tail,
            batch=batch, seq=seq, seqs=1, rows=MIXER_TOKENS, chunk=GMLP_CHUNK, pos0=0,
            emit_vn=False)
        xp = _peer(h2t, r2, cnt, e1, e2, u_b, vt_b, x1, gate2_p, g_final, tn=PEER_TOKENS,
                   tokens_per_gate=seq, final_norm=final)

        hist_s = jnp.pad(cache_pool[l], ((0, 0), (POOL_HIST_PAD - cache_pool.shape[2], 0), (0, 0)))
        ws_s = jnp.tile(gmlp_ws[l][:, :dec_seq, :dec_seq], (1, reps, reps))
        bs_s = jnp.tile(gmlp_bs[l][:, :dec_seq], (1, reps)).T
        x1, h2t, r2, cnt, e1, e2, hist_sn, vn = _mixer(
            xs, mod_s, hist_s, *common, ws_s, bs_s, *tail,
            batch=dec_batch, seq=dec_seq, seqs=dec_batch, rows=dec_seq, chunk=dec_seq,
            pos0=past_len, emit_vn=True)
        xs = _peer(h2t, r2, cnt, e1, e2, u_b, vt_b, x1, gate2_s, g_final, tn=n_s,
                   tokens_per_gate=1, final_norm=final)

        pool_p.append(hist_p[:, 1:])
        pool_s.append(hist_sn[:, 1:])
        vn_s.append(vn.reshape(dec_batch, dec_seq, -1))

    return (xp.reshape(batch, seq, d), xs.reshape(dec_batch, dec_seq, d),
            jnp.stack(pool_p), jnp.stack(pool_s), jnp.stack(vn_s))
```
